```python
import jax, jax.numpy as jnp
from jax import lax
import numpy as np

D_MODEL = 2048
BATCH = 4
SEQ = 4096
DEPTH = 4

GRID_W = 64
CTX_LEN = 256
CHUNK = 128
ROWS_PER_CHUNK = CHUNK // GRID_W
MIX_W = D_MODEL
A_WIDTH = MIX_W // 2
A_HEADS = 8
A_HEAD_DIM = A_WIDTH // A_HEADS
B_WIDTH = MIX_W - A_WIDTH
B_HEADS = 4
B_DV = B_WIDTH // B_HEADS
B_DK = B_DV // 2
B_KEY_W = B_HEADS * B_DK
GATE_RANK = 16
GATE_TAU = 16.0
FFN_HIDDEN = -(-8 * D_MODEL // (3 * 256)) * 256
P_IN = 2 * A_WIDTH + 2 * B_KEY_W + 2 * B_WIDTH + 2 * GATE_RANK
EPS = 1e-6

kernel_name = 'hybrid_gmlp_gla_prefix_dit'


def rms_norm(x, g):
    xf = x.astype(jnp.float32)
    y = xf * lax.rsqrt(jnp.mean(xf * xf, axis=-1, keepdims=True) + EPS)
    return (y * g.astype(jnp.float32)).astype(x.dtype)


def layer_norm(x, g, b):
    xf = x.astype(jnp.float32)
    mu = jnp.mean(xf, axis=-1, keepdims=True)
    var = jnp.mean(jnp.square(xf - mu), axis=-1, keepdims=True)
    y = (xf - mu) * lax.rsqrt(var + EPS)
    return (y * g.astype(jnp.float32) + b.astype(jnp.float32)).astype(x.dtype)


def modulate(h, shift, scale):
    return h * (1 + scale) + shift


def split_proj(p):
    sizes = (A_WIDTH, A_WIDTH, B_KEY_W, B_KEY_W, B_WIDTH, B_WIDTH, GATE_RANK, GATE_RANK)
    out, start = [], 0
    for s in sizes:
        out.append(p[..., start:start + s])
        start += s
    return out


def spatial_gating(u, v, ln_g, ln_b, w_s, b_s, n_chunks):
    bsz, length, _ = u.shape
    u = jax.nn.gelu(u).reshape(bsz, n_chunks, CHUNK, A_HEADS, A_HEAD_DIM)
    v = jax.nn.gelu(v).reshape(bsz, length, A_HEADS, A_HEAD_DIM)
    v = layer_norm(v, ln_g.reshape(A_HEADS, A_HEAD_DIM), ln_b.reshape(A_HEADS, A_HEAD_DIM))
    v = v.reshape(bsz, n_chunks, CHUNK, A_HEADS, A_HEAD_DIM)
    mixed = jnp.einsum('hij,bnjhd->bnihd', w_s, v) + b_s.T[None, None, :, :, None]
    return (u * mixed).reshape(bsz, length, A_WIDTH)


def gla_prep(q, k, v, d_f, d_b, wd2_f, bd_f, wd2_b, bd_b):
    bsz, length, _ = q.shape
    def heads(t, d):
        return t.astype(jnp.float32).reshape(bsz, length, B_HEADS, d)
    q = heads(q, B_DK) * (B_DK ** -0.5)
    k = heads(k, B_DK)
    v = heads(v, B_DV)
    la_f = heads(jax.nn.log_sigmoid((d_f @ wd2_f + bd_f).astype(jnp.float32)) / GATE_TAU, B_DK)
    la_b = heads(jax.nn.log_sigmoid((d_b @ wd2_b + bd_b).astype(jnp.float32)) / GATE_TAU, B_DK)
    return q, k, v, la_f, la_b


def gla_scan(q, k, v, log_a, s0):
    bsz, length, nh, _ = q.shape
    n = length // CHUNK
    def to_chunks(t):
        return t.reshape(bsz, n, CHUNK, nh, t.shape[-1]).transpose(1, 0, 3, 2, 4)
    tri = jnp.tril(jnp.ones((CHUNK, CHUNK), dtype=bool))[None, None, :, :, None]
    def step(s, inp):
        qc, kc, vc, gc = inp
        b = jnp.cumsum(gc, axis=2)
        b_last = b[:, :, -1:, :]
        rel = jnp.where(tri, b[:, :, :, None, :] - b[:, :, None, :, :], -jnp.inf)
        scores = jnp.einsum('bhid,bhjd,bhijd->bhij', qc, kc, jnp.exp(rel))
        o = jnp.einsum('bhij,bhje->bhie', scores, vc) + jnp.einsum('bhid,bhde->bhie', qc * jnp.exp(b), s)
        s = jnp.exp(b_last[:, :, 0, :, None]) * s + jnp.einsum('bhjd,bhje->bhde', kc * jnp.exp(b_last - b), vc)
        return s, o
    s_fin, o = lax.scan(step, s0, (to_chunks(q), to_chunks(k), to_chunks(v), to_chunks(log_a)))
    o = o.transpose(1, 0, 3, 2, 4).reshape(bsz, length, nh, -1)
    return o, s_fin


def bidir_gla(px, pc):
    qx, kx, vx, fx, bx = px
    qc, kc, vc, fc, bc = pc
    s0 = jnp.zeros((qx.shape[0], B_HEADS, B_DK, B_DV), jnp.float32)
    flip = lambda t: jnp.flip(t, axis=1)
    oc_f, sc_f = gla_scan(qc, kc, vc, fc, s0)
    ox_f, _ = gla_scan(qx, kx, vx, fx, sc_f)
    oc_b, sc_b = gla_scan(flip(qc), flip(kc), flip(vc), flip(bc), s0)
    ox_b, _ = gla_scan(flip(qx), flip(kx), flip(vx), flip(bx), sc_b)
    return ox_f + flip(ox_b), oc_f + flip(oc_b)


def gla_out(o, g, out_g):
    bsz, length = o.shape[:2]
    y = o * lax.rsqrt(jnp.mean(o * o, axis=-1, keepdims=True) + EPS) * out_g.astype(jnp.float32).reshape(B_HEADS, B_DV)
    return y.reshape(bsz, length, B_WIDTH).astype(g.dtype) * jax.nn.silu(g)


def swiglu(h, w_in, w_out):
    a, b = jnp.split(h @ w_in, 2, axis=-1)
    return (jax.nn.silu(a) * b) @ w_out


def trunk_layer(x, xc, c_act, cc_act, w_mod, b_mod, g_pre_mix, g_post_mix, g_pre_ffn, g_post_ffn, w_in, ln_g, ln_b, w_s, b_s, wd2_f, bd_f, wd2_b, bd_b, out_g, w_out, w_ffn_in, w_ffn_out, rows, update_ctx):
    mod_x = jnp.split((c_act @ w_mod + b_mod)[:, None, :], 6, axis=-1)
    mod_c = jnp.split(cc_act @ w_mod + b_mod, 6, axis=-1)
    hx = modulate(rms_norm(x, g_pre_mix), mod_x[0], mod_x[1])
    hc = modulate(rms_norm(xc, g_pre_mix), mod_c[0], mod_c[1])
    ux, vx, qx, kx, vvx, gx, dfx, dbx = split_proj(hx @ w_in)
    uc, vc, qc, kc, vvc, gc, dfc, dbc = split_proj(hc @ w_in)
    gla_x, gla_c = bidir_gla(gla_prep(qx, kx, vvx, dfx, dbx, wd2_f, bd_f, wd2_b, bd_b),
                             gla_prep(qc, kc, vvc, dfc, dbc, wd2_f, bd_f, wd2_b, bd_b))
    a_x = spatial_gating(ux, vx, ln_g, ln_b, w_s, b_s, rows // ROWS_PER_CHUNK)
    y_x = jnp.concatenate([a_x, gla_out(gla_x, gx, out_g)], axis=-1) @ w_out
    x = x + mod_x[2] * rms_norm(y_x, g_post_mix)
    f_x = swiglu(modulate(rms_norm(x, g_pre_ffn), mod_x[3], mod_x[4]), w_ffn_in, w_ffn_out)
    x = x + mod_x[5] * rms_norm(f_x, g_post_ffn)
    if update_ctx:
        a_c = spatial_gating(uc, vc, ln_g, ln_b, w_s, b_s, xc.shape[1] // CHUNK)
        y_c = jnp.concatenate([a_c, gla_out(gla_c, gc, out_g)], axis=-1) @ w_out
        xc = xc + mod_c[2] * rms_norm(y_c, g_post_mix)
        f_c = swiglu(modulate(rms_norm(xc, g_pre_ffn), mod_c[3], mod_c[4]), w_ffn_in, w_ffn_out)
        xc = xc + mod_c[5] * rms_norm(f_c, g_post_ffn)
    return x, xc


def setup_inputs(seed: int = 0) -> dict:
    key = jax.random.key(seed)
    ks = iter(jax.random.split(key, 32))
    f32 = jnp.float32
    def nrm(shape, scale=1.0):
        return jax.random.normal(next(ks), shape, f32) * scale
    def gain(shape):
        return 1.0 + nrm(shape, 0.05)
    L = DEPTH
    return {
        'x': nrm((BATCH, SEQ, D_MODEL)),
        'c': nrm((BATCH, D_MODEL)),
        'ctx': nrm((BATCH, CTX_LEN, D_MODEL)),
        'c_ctx': nrm((D_MODEL,)),
        'w_mod': nrm((L, D_MODEL, 6 * D_MODEL), 0.5 * D_MODEL ** -0.5),
        'b_mod': nrm((L, 6 * D_MODEL), 0.01),
        'g_pre_mix': gain((L, D_MODEL)),
        'g_post_mix': gain((L, D_MODEL)),
        'g_pre_ffn': gain((L, D_MODEL)),
        'g_post_ffn': gain((L, D_MODEL)),
        'w_in': nrm((L, D_MODEL, P_IN), D_MODEL ** -0.5),
        'gmlp_ln_g': gain((L, A_WIDTH)),
        'gmlp_ln_b': nrm((L, A_WIDTH), 0.01),
        'gmlp_ws': nrm((L, A_HEADS, CHUNK, CHUNK), CHUNK ** -0.5),
        'gmlp_bs': gain((L, A_HEADS, CHUNK)),
        'gla_wd2_fwd': nrm((L, GATE_RANK, B_KEY_W), GATE_RANK ** -0.5),
        'gla_bd_fwd': nrm((L, B_KEY_W), 0.1),
        'gla_wd2_bwd': nrm((L, GATE_RANK, B_KEY_W), GATE_RANK ** -0.5),
        'gla_bd_bwd': nrm((L, B_KEY_W), 0.1),
        'gla_out_g': gain((L, B_WIDTH)),
        'w_out': nrm((L, MIX_W, D_MODEL), MIX_W ** -0.5),
        'w_ffn_in': nrm((L, D_MODEL, 2 * FFN_HIDDEN), D_MODEL ** -0.5),
        'w_ffn_out': nrm((L, FFN_HIDDEN, D_MODEL), FFN_HIDDEN ** -0.5),
    }


def reference(x, c, ctx, c_ctx, w_mod, b_mod, g_pre_mix, g_post_mix, g_pre_ffn, g_post_ffn, w_in, gmlp_ln_g, gmlp_ln_b, gmlp_ws, gmlp_bs, gla_wd2_fwd, gla_bd_fwd, gla_wd2_bwd, gla_bd_bwd, gla_out_g, w_out, w_ffn_in, w_ffn_out):
    rows = x.shape[1] // GRID_W
    c_act = jax.nn.silu(c)
    cc_act = jax.nn.silu(c_ctx)
    xc = ctx
    for i in range(DEPTH):
        x, xc = trunk_layer(x, xc, c_act, cc_act, w_mod[i], b_mod[i], g_pre_mix[i], g_post_mix[i], g_pre_ffn[i], g_post_ffn[i], w_in[i], gmlp_ln_g[i], gmlp_ln_b[i], gmlp_ws[i], gmlp_bs[i], gla_wd2_fwd[i], gla_bd_fwd[i], gla_wd2_bwd[i], gla_bd_bwd[i], gla_out_g[i], w_out[i], w_ffn_in[i], w_ffn_out[i], rows, i < DEPTH - 1)
    return x
```

```python
import functools

import jax
import jax.numpy as jnp
from jax import lax
from jax.experimental import pallas as pl
from jax.experimental.pallas import tpu as pltpu

F32 = jnp.float32
BF16 = jnp.bfloat16

D_MODEL = 2048
CHUNK = 128
A_WIDTH = 1024
A_HEADS = 8
A_HEAD_DIM = A_WIDTH // A_HEADS
B_WIDTH = 1024
B_HEADS = 4
B_DV = B_WIDTH // B_HEADS
B_DK = B_DV // 2
B_KEY_W = B_HEADS * B_DK
GATE_RANK = 16
GATE_TAU = 16.0
FFN_HIDDEN = 5632
EPS = 1e-6

COL_U = 0
COL_V = A_WIDTH
COL_Q = 2 * A_WIDTH
COL_K = COL_Q + B_KEY_W
COL_VV = COL_K + B_KEY_W
COL_G = COL_VV + B_WIDTH
COL_D = COL_G + B_WIDTH
P_COLS = COL_D + 2 * GATE_RANK
P_PAD = 5376
D_BLOCK = 128

IN_TM = 1024
IN_TN = 768
MIX_TM = 512
FFN_TM = 512
FFN_TH = 512
GMLP_ROWS = 512
MOD_TN = 1024
MOD_ROWS = 8
ROW_BLK = 128

VMEM_LIMIT = 56 * 1024 * 1024


def _params(*sem):
    return pltpu.CompilerParams(dimension_semantics=sem, vmem_limit_bytes=VMEM_LIMIT)


def _row_loop(n_rows, body):
    def step(i, carry):
        body(pl.multiple_of(i * ROW_BLK, ROW_BLK))
        return carry
    lax.fori_loop(0, n_rows // ROW_BLK, step, 0)


def _silu(x):
    return x * jax.nn.sigmoid(x)


def _gelu_tanh(x):
    k = 0.7978845608028654
    return x * (0.5 * (1.0 + jnp.tanh(k * (x + 0.044715 * (x * x * x)))))


def _rms(x):
    return x * lax.rsqrt(jnp.mean(x * x, axis=-1, keepdims=True) + EPS)


def _mod_kernel(act_ref, w_ref, b_ref, o_ref):
    a = _silu(act_ref[...])
    o_ref[...] = jnp.dot(a.astype(BF16), w_ref[...].astype(BF16),
                         preferred_element_type=F32) + b_ref[...]


def _mod_call(act, w_mod, b_mod):
    n_layers, d, n = w_mod.shape
    return pl.pallas_call(
        _mod_kernel,
        grid=(n_layers, n // MOD_TN),
        in_specs=[
            pl.BlockSpec((MOD_ROWS, d), lambda l, j: (0, 0)),
            pl.BlockSpec((None, d, MOD_TN), lambda l, j: (l, 0, j)),
            pl.BlockSpec((None, 1, MOD_TN), lambda l, j: (l, 0, j)),
        ],
        out_specs=pl.BlockSpec((None, MOD_ROWS, MOD_TN), lambda l, j: (l, 0, j)),
        out_shape=jax.ShapeDtypeStruct((n_layers, MOD_ROWS, n), F32),
        compiler_params=_params("arbitrary", "arbitrary"),
        name="mod",
    )(act, w_mod, b_mod.reshape(n_layers, 1, n))


def _inproj_kernel(x_ref, mod_ref, g_ref, w_ref, o_ref, h_ref):
    @pl.when(pl.program_id(1) == 0)
    def _():
        def body(r):
            rows = pl.ds(r, ROW_BLK)
            y = _rms(x_ref[rows, :]) * g_ref[...]
            h = y * (1.0 + mod_ref[1:2, :]) + mod_ref[0:1, :]
            h_ref[rows, :] = h.astype(BF16)
        _row_loop(IN_TM, body)

    o_ref[...] = jnp.dot(h_ref[...], w_ref[...], preferred_element_type=F32)


def _inproj_call(xs, mod, g_pre, w_in, layer, mod_row):
    n_tok, d = xs.shape
    return pl.pallas_call(
        _inproj_kernel,
        grid=(n_tok // IN_TM, P_PAD // IN_TN),
        in_specs=[
            pl.BlockSpec((IN_TM, d), lambda i, j: (i, 0)),
            pl.BlockSpec((None, None, 6, d), lambda i, j: (layer, mod_row(i, IN_TM), 0, 0)),
            pl.BlockSpec((None, 1, d), lambda i, j: (layer, 0, 0)),
            pl.BlockSpec((None, d, IN_TN), lambda i, j: (layer, 0, j)),
        ],
        out_specs=pl.BlockSpec((IN_TM, IN_TN), lambda i, j: (i, j)),
        out_shape=jax.ShapeDtypeStruct((n_tok, P_PAD), F32),
        scratch_shapes=[pltpu.VMEM((IN_TM, d), BF16)],
        compiler_params=_params("arbitrary", "arbitrary"),
        name="inproj",
    )(xs, mod, g_pre, w_in)


def _gmlp_kernel(u_ref, v_ref, lng_ref, lnb_ref, ws_ref, bs_ref, o_ref):
    def body(r):
        rows = pl.ds(r, CHUNK)
        for h in range(A_HEADS):
            cols = slice(h * A_HEAD_DIM, (h + 1) * A_HEAD_DIM)
            v = _gelu_tanh(v_ref[rows, cols])
            mu = jnp.mean(v, axis=-1, keepdims=True)
            vc = v - mu
            var = jnp.mean(vc * vc, axis=-1, keepdims=True)
            vn = vc * lax.rsqrt(var + EPS) * lng_ref[:, cols] + lnb_ref[:, cols]
            mixed = jnp.dot(ws_ref[h], vn.astype(BF16), preferred_element_type=F32)
            mixed = mixed + bs_ref[:, h:h + 1]
            u = _gelu_tanh(u_ref[rows, cols])
            o_ref[rows, cols] = (u * mixed).astype(BF16)
    _row_loop(GMLP_ROWS, body)


def _gmlp_call(p, ln_g, ln_b, w_s, b_s, layer):
    n_tok = p.shape[0]
    return pl.pallas_call(
        _gmlp_kernel,
        grid=(n_tok // GMLP_ROWS,),
        in_specs=[
            pl.BlockSpec((GMLP_ROWS, A_WIDTH), lambda i: (i, COL_U // A_WIDTH)),
            pl.BlockSpec((GMLP_ROWS, A_WIDTH), lambda i: (i, COL_V // A_WIDTH)),
            pl.BlockSpec((None, 1, A_WIDTH), lambda i: (layer, 0, 0)),
            pl.BlockSpec((None, 1, A_WIDTH), lambda i: (layer, 0, 0)),
            pl.BlockSpec((None, A_HEADS, CHUNK, CHUNK), lambda i: (layer, 0, 0, 0)),
            pl.BlockSpec((None, CHUNK, A_HEADS), lambda i: (layer, 0, 0)),
        ],
        out_specs=pl.BlockSpec((GMLP_ROWS, A_WIDTH), lambda i: (i, 0)),
        out_shape=jax.ShapeDtypeStruct((n_tok, A_WIDTH), BF16),
        compiler_params=_params("arbitrary"),
        name="gmlp",
    )(p, p, ln_g, ln_b, w_s, b_s)


def _log_sigmoid(z):
    return -(jnp.maximum(-z, 0.0) + jnp.log1p(jnp.exp(-jnp.abs(z))))


def _cumsum_rows(tri, la):
    h1 = la.astype(BF16)
    r1 = la - h1.astype(F32)
    h2 = r1.astype(BF16)
    h3 = (r1 - h2.astype(F32)).astype(BF16)
    dot = functools.partial(jnp.dot, preferred_element_type=F32)
    return dot(tri, h1) + dot(tri, h2) + dot(tri, h3)


def _gla_direction(q_ref, k_ref, v_ref, d_ref, wd_ref, bd_ref, s_ref, o_ref, reverse):
    row = lax.broadcasted_iota(jnp.int32, (CHUNK, CHUNK), 0)
    col = lax.broadcasted_iota(jnp.int32, (CHUNK, CHUNK), 1)
    causal = (col >= row) if reverse else (col <= row)
    eye = col == row
    tri = jnp.where(causal, 1.0, 0.0).astype(BF16)

    z = jnp.dot(d_ref[...].astype(BF16), wd_ref[...], preferred_element_type=F32) + bd_ref[...]
    la = _log_sigmoid(z) * (1.0 / GATE_TAU)
    b = _cumsum_rows(tri, la)
    end = 0 if reverse else CHUNK - 1
    b_end = b[end:end + 1, :]
    b_mid = b[CHUNK // 2:CHUNK // 2 + 1, :]

    for h in range(B_HEADS):
        kc = slice(h * B_DK, (h + 1) * B_DK)
        vc = slice(h * B_DV, (h + 1) * B_DV)
        bh = b[:, kc]
        q = q_ref[:, kc] * (B_DK ** -0.5)
        k = k_ref[:, kc]
        v = v_ref[:, vc].astype(BF16)
        q_in = (q * jnp.exp(bh)).astype(BF16)
        q_t = (q * jnp.exp(bh - b_mid[:, kc])).astype(BF16)
        k_t = (k * jnp.exp(b_mid[:, kc] - bh)).astype(BF16)
        k_d = k * jnp.exp(b_end[:, kc] - bh)
        scores = lax.dot_general(q_t, k_t, (((1,), (1,)), ((), ())),
                                 preferred_element_type=F32)
        scores = jnp.where(causal, scores, 0.0).astype(BF16)
        s = s_ref[h]
        o = jnp.dot(scores, v, preferred_element_type=F32)
        o = o + jnp.dot(q_in, s.astype(BF16), preferred_element_type=F32)
        o_ref[:, vc] = o
        decay = jnp.broadcast_to(jnp.exp(b_end[:, kc]), (B_DK, B_DK))
        decay_col = jnp.sum(jnp.where(eye, decay, 0.0), axis=-1, keepdims=True)
        s_ref[h] = decay_col * s + jnp.dot(k_d.T.astype(BF16), v, preferred_element_type=F32)


def _gla_kernel(qf_ref, kf_ref, vf_ref, df_ref, qb_ref, kb_ref, vb_ref, db_ref,
                wdf_ref, bdf_ref, wdb_ref, bdb_ref, of_ref, ob_ref, s_ref):
    @pl.when(pl.program_id(1) == 0)
    def _():
        s_ref[...] = jnp.zeros_like(s_ref)

    _gla_direction(qf_ref, kf_ref, vf_ref, df_ref, wdf_ref, bdf_ref, s_ref.at[0], of_ref, False)
    _gla_direction(qb_ref, kb_ref, vb_ref, db_ref, wdb_ref, bdb_ref, s_ref.at[1], ob_ref, True)


def _gla_call(p, wd_f, bd_f, wd_b, bd_b, layer, batch, n_x, n_c):
    n_tok = p.shape[0]

    def fwd(b, s):
        return jnp.where(s < n_c, batch * n_x + b * n_c + s, b * n_x + (s - n_c))

    def bwd(b, s):
        return jnp.where(s < n_c, batch * n_x + b * n_c + (n_c - 1 - s),
                         b * n_x + (n_x - 1 - (s - n_c)))

    def chunk_specs(order):
        return [
            pl.BlockSpec((CHUNK, B_KEY_W), lambda b, s: (order(b, s), COL_Q // B_KEY_W)),
            pl.BlockSpec((CHUNK, B_KEY_W), lambda b, s: (order(b, s), COL_K // B_KEY_W)),
            pl.BlockSpec((CHUNK, B_WIDTH), lambda b, s: (order(b, s), COL_VV // B_WIDTH)),
            pl.BlockSpec((CHUNK, D_BLOCK), lambda b, s: (order(b, s), COL_D // D_BLOCK)),
        ]

    def weight_specs():
        return [
            pl.BlockSpec((None, D_BLOCK, B_KEY_W), lambda b, s: (layer, 0, 0)),
            pl.BlockSpec((None, 1, B_KEY_W), lambda b, s: (layer, 0, 0)),
        ]

    out = jax.ShapeDtypeStruct((n_tok, B_WIDTH), F32)
    return pl.pallas_call(
        _gla_kernel,
        grid=(batch, n_x + n_c),
        in_specs=chunk_specs(fwd) + chunk_specs(bwd) + weight_specs() + weight_specs(),
        out_specs=[
            pl.BlockSpec((CHUNK, B_WIDTH), lambda b, s: (fwd(b, s), 0)),
            pl.BlockSpec((CHUNK, B_WIDTH), lambda b, s: (bwd(b, s), 0)),
        ],
        out_shape=[out, out],
        scratch_shapes=[pltpu.VMEM((2, B_HEADS, B_DK, B_DV), F32)],
        compiler_params=_params("arbitrary", "arbitrary"),
        name="gla",
    )(p, p, p, p, p, p, p, p, wd_f, bd_f, wd_b, bd_b)


def _mixout_kernel(a_ref, of_ref, ob_ref, g_ref, x_ref, mod_ref, og_ref, gpost_ref,
                   wo_ref, o_ref, gla_ref, y_ref):
    def head_norm(r):
        rows = pl.ds(r, ROW_BLK)
        for h in range(B_HEADS):
            cols = slice(h * B_DV, (h + 1) * B_DV)
            o = of_ref[rows, cols] + ob_ref[rows, cols]
            y = _rms(o) * og_ref[:, cols]
            gla_ref[rows, cols] = (y * _silu(g_ref[rows, cols])).astype(BF16)
    _row_loop(MIX_TM, head_norm)

    y_ref[...] = (
        jnp.dot(a_ref[...], wo_ref[0:A_WIDTH, :], preferred_element_type=F32)
        + jnp.dot(gla_ref[...], wo_ref[A_WIDTH:A_WIDTH + B_WIDTH, :], preferred_element_type=F32))

    def residual(r):
        rows = pl.ds(r, ROW_BLK)
        y = _rms(y_ref[rows, :]) * gpost_ref[...]
        o_ref[rows, :] = x_ref[rows, :] + mod_ref[2:3, :] * y
    _row_loop(MIX_TM, residual)


def _mixout_call(a, o_f, o_b, p, xs, mod, out_g, g_post, w_out, layer, mod_row):
    n_tok, d = xs.shape
    return pl.pallas_call(
        _mixout_kernel,
        grid=(n_tok // MIX_TM,),
        in_specs=[
            pl.BlockSpec((MIX_TM, A_WIDTH), lambda i: (i, 0)),
            pl.BlockSpec((MIX_TM, B_WIDTH), lambda i: (i, 0)),
            pl.BlockSpec((MIX_TM, B_WIDTH), lambda i: (i, 0)),
            pl.BlockSpec((MIX_TM, B_WIDTH), lambda i: (i, COL_G // B_WIDTH)),
            pl.BlockSpec((MIX_TM, d), lambda i: (i, 0)),
            pl.BlockSpec((None, None, 6, d), lambda i: (layer, mod_row(i, MIX_TM), 0, 0)),
            pl.BlockSpec((None, 1, B_WIDTH), lambda i: (layer, 0, 0)),
            pl.BlockSpec((None, 1, d), lambda i: (layer, 0, 0)),
            pl.BlockSpec((None, A_WIDTH + B_WIDTH, d), lambda i: (layer, 0, 0)),
        ],
        out_specs=pl.BlockSpec((MIX_TM, d), lambda i: (i, 0)),
        out_shape=jax.ShapeDtypeStruct((n_tok, d), F32),
        scratch_shapes=[pltpu.VMEM((MIX_TM, B_WIDTH), BF16), pltpu.VMEM((MIX_TM, d), F32)],
        input_output_aliases={4: 0},
        compiler_params=_params("arbitrary"),
        name="mixout",
    )(a, o_f, o_b, p, xs, mod, out_g, g_post, w_out)


def _ffn_kernel(x_ref, mod_ref, gpre_ref, gpost_ref, wa_ref, wb_ref, wo_ref, o_ref, h_ref):
    j = pl.program_id(1)

    @pl.when(j == 0)
    def _():
        def body(r):
            rows = pl.ds(r, ROW_BLK)
            y = _rms(x_ref[rows, :]) * gpre_ref[...]
            h = y * (1.0 + mod_ref[4:5, :]) + mod_ref[3:4, :]
            h_ref[rows, :] = h.astype(BF16)
        _row_loop(FFN_TM, body)

    h = h_ref[...]
    a = jnp.dot(h, wa_ref[...], preferred_element_type=F32)
    b = jnp.dot(h, wb_ref[...], preferred_element_type=F32)
    act = (_silu(a) * b).astype(BF16)
    part = jnp.dot(act, wo_ref[...], preferred_element_type=F32)

    @pl.when(j == 0)
    def _():
        o_ref[...] = part

    @pl.when(j > 0)
    def _():
        o_ref[...] += part

    @pl.when(j == pl.num_programs(1) - 1)
    def _():
        def body(r):
            rows = pl.ds(r, ROW_BLK)
            y = _rms(o_ref[rows, :]) * gpost_ref[...]
            o_ref[rows, :] = x_ref[rows, :] + mod_ref[5:6, :] * y
        _row_loop(FFN_TM, body)


def _ffn_call(xs, mod, g_pre, g_post, w_ffn_in, w_ffn_out, layer, mod_row):
    n_tok, d = xs.shape
    n_h = FFN_HIDDEN // FFN_TH
    return pl.pallas_call(
        _ffn_kernel,
        grid=(n_tok // FFN_TM, n_h),
        in_specs=[
            pl.BlockSpec((FFN_TM, d), lambda i, j: (i, 0)),
            pl.BlockSpec((None, None, 6, d), lambda i, j: (layer, mod_row(i, FFN_TM), 0, 0)),
            pl.BlockSpec((None, 1, d), lambda i, j: (layer, 0, 0)),
            pl.BlockSpec((None, 1, d), lambda i, j: (layer, 0, 0)),
            pl.BlockSpec((None, d, FFN_TH), lambda i, j: (layer, 0, j)),
            pl.BlockSpec((None, d, FFN_TH), lambda i, j: (layer, 0, n_h + j)),
            pl.BlockSpec((None, FFN_TH, d), lambda i, j: (layer, j, 0)),
        ],
        out_specs=pl.BlockSpec((FFN_TM, d), lambda i, j: (i, 0)),
        out_shape=jax.ShapeDtypeStruct((n_tok, d), F32),
        scratch_shapes=[pltpu.VMEM((FFN_TM, d), BF16)],
        input_output_aliases={0: 0},
        compiler_params=_params("arbitrary", "arbitrary"),
        name="ffn",
    )(xs, mod, g_pre, g_post, w_ffn_in, w_ffn_in, w_ffn_out)


def kernel(x, c, ctx, c_ctx, w_mod, b_mod, g_pre_mix, g_post_mix, g_pre_ffn, g_post_ffn, w_in, gmlp_ln_g, gmlp_ln_b, gmlp_ws, gmlp_bs, gla_wd2_fwd, gla_bd_fwd, gla_wd2_bwd, gla_bd_bwd, gla_out_g, w_out, w_ffn_in, w_ffn_out):
    batch, seq, d = x.shape
    ctx_len = ctx.shape[1]
    n_layers = w_mod.shape[0]
    n_lat = batch * seq
    n_ctx = batch * ctx_len
    assert d == D_MODEL and batch + 1 <= MOD_ROWS
    assert seq % IN_TM == 0 and n_ctx % IN_TM == 0 and ctx_len % CHUNK == 0

    def mod_row(i, tm):
        return jnp.where(i < n_lat // tm, i // (seq // tm), batch)

    xs = jnp.concatenate([x.reshape(n_lat, d), ctx.reshape(n_ctx, d)], axis=0)

    act = jnp.zeros((MOD_ROWS, d), F32).at[:batch].set(c).at[batch].set(c_ctx)
    mod = _mod_call(act, w_mod, b_mod).reshape(n_layers, MOD_ROWS, 6, d)

    row = lambda t: t.reshape(n_layers, 1, t.shape[-1])
    w_in_b = jnp.pad(w_in, ((0, 0), (0, 0), (0, P_PAD - P_COLS))).astype(BF16)
    w_out_b = w_out.astype(BF16)
    w_ffn_in_b = w_ffn_in.astype(BF16)
    w_ffn_out_b = w_ffn_out.astype(BF16)
    w_s_b = gmlp_ws.astype(BF16)
    b_s_t = jnp.swapaxes(gmlp_bs, 1, 2)
    wd_f = jnp.pad(gla_wd2_fwd, ((0, 0), (0, D_BLOCK - GATE_RANK), (0, 0))).astype(BF16)
    wd_b = jnp.pad(gla_wd2_bwd, ((0, 0), (GATE_RANK, D_BLOCK - 2 * GATE_RANK), (0, 0))).astype(BF16)

    for l in range(n_layers):
        p = _inproj_call(xs, mod, row(g_pre_mix), w_in_b, l, mod_row)
        a = _gmlp_call(p, row(gmlp_ln_g), row(gmlp_ln_b), w_s_b, b_s_t, l)
        o_f, o_b = _gla_call(p, wd_f, row(gla_bd_fwd), wd_b, row(gla_bd_bwd), l,
                             batch, seq // CHUNK, ctx_len // CHUNK)
        xs = _mixout_call(a, o_f, o_b, p, xs, mod, row(gla_out_g), row(g_post_mix),
                          w_out_b, l, mod_row)
        xs = _ffn_call(xs, mod, row(g_pre_ffn), row(g_post_ffn), w_ffn_in_b, w_ffn_out_b,
                       l, mod_row)
    return xs[:n_lat].reshape(batch, seq, d)
```

```python
import functools

import jax
import jax.numpy as jnp
from jax import lax
from jax.experimental import pallas as pl
from jax.experimental.pallas import tpu as pltpu

F32 = jnp.float32
BF16 = jnp.bfloat16

D_MODEL = 2048
CHUNK = 128
A_WIDTH = 1024
A_HEADS = 8
A_HEAD_DIM = A_WIDTH // A_HEADS
B_WIDTH = 1024
B_HEADS = 4
B_DV = B_WIDTH // B_HEADS
B_DK = B_DV // 2
B_KEY_W = B_HEADS * B_DK
GATE_RANK = 16
GATE_TAU = 16.0
FFN_HIDDEN = 5632
EPS = 1e-6

COL_U = 0
COL_V = A_WIDTH
COL_Q = 2 * A_WIDTH
COL_K = COL_Q + B_KEY_W
COL_VV = COL_K + B_KEY_W
COL_G = COL_VV + B_WIDTH
COL_D = COL_G + B_WIDTH
P_COLS = COL_D + 2 * GATE_RANK
P_PAD = 5376
D_BLOCK = 128

IN_TM = 1024
IN_TN = 768
MIX_TM = 512
FFN_TM = 512
FFN_TH = 512
FFN_TN = 512
FFN_NH = FFN_HIDDEN // FFN_TH
FFN_NO = D_MODEL // FFN_TN
GMLP_ROWS = 512
MOD_TN = 1024
MOD_ROWS = 8
ROW_BLK = 128

VMEM_LIMIT = 56 * 1024 * 1024


def _params(*sem):
    return pltpu.CompilerParams(dimension_semantics=sem, vmem_limit_bytes=VMEM_LIMIT)


def _row_loop(n_rows, body):
    def step(i, carry):
        body(pl.multiple_of(i * ROW_BLK, ROW_BLK))
        return carry
    lax.fori_loop(0, n_rows // ROW_BLK, step, 0)


def _silu(x):
    return x * jax.nn.sigmoid(x)


def _gelu_tanh(x):
    k = 0.7978845608028654
    return x * (0.5 * (1.0 + jnp.tanh(k * (x + 0.044715 * (x * x * x)))))


def _rms(x):
    return x * lax.rsqrt(jnp.mean(x * x, axis=-1, keepdims=True) + EPS)


def _mod_kernel(act_ref, w_ref, b_ref, o_ref):
    a = _silu(act_ref[...])
    o_ref[...] = jnp.dot(a.astype(BF16), w_ref[...].astype(BF16),
                         preferred_element_type=F32) + b_ref[...]


def _mod_call(act, w_mod, b_mod):
    n_layers, d, n = w_mod.shape
    return pl.pallas_call(
        _mod_kernel,
        grid=(n_layers, n // MOD_TN),
        in_specs=[
            pl.BlockSpec((MOD_ROWS, d), lambda l, j: (0, 0)),
            pl.BlockSpec((None, d, MOD_TN), lambda l, j: (l, 0, j)),
            pl.BlockSpec((None, 1, MOD_TN), lambda l, j: (l, 0, j)),
        ],
        out_specs=pl.BlockSpec((None, MOD_ROWS, MOD_TN), lambda l, j: (l, 0, j)),
        out_shape=jax.ShapeDtypeStruct((n_layers, MOD_ROWS, n), F32),
        compiler_params=_params("arbitrary", "arbitrary"),
        name="mod",
    )(act, w_mod, b_mod.reshape(n_layers, 1, n))


def _inproj_kernel(x_ref, mod_ref, g_ref, w_ref, o_ref, h_ref):
    @pl.when(pl.program_id(1) == 0)
    def _():
        def body(r):
            rows = pl.ds(r, ROW_BLK)
            y = _rms(x_ref[rows, :]) * g_ref[...]
            h = y * (1.0 + mod_ref[1:2, :]) + mod_ref[0:1, :]
            h_ref[rows, :] = h.astype(BF16)
        _row_loop(IN_TM, body)

    o_ref[...] = jnp.dot(h_ref[...], w_ref[...], preferred_element_type=F32).astype(BF16)


def _inproj_call(xs, mod, g_pre, w_in, layer, mod_row):
    n_tok, d = xs.shape
    return pl.pallas_call(
        _inproj_kernel,
        grid=(n_tok // IN_TM, P_PAD // IN_TN),
        in_specs=[
            pl.BlockSpec((IN_TM, d), lambda i, j: (i, 0)),
            pl.BlockSpec((None, None, 6, d), lambda i, j: (layer, mod_row(i, IN_TM), 0, 0)),
            pl.BlockSpec((None, 1, d), lambda i, j: (layer, 0, 0)),
            pl.BlockSpec((None, d, IN_TN), lambda i, j: (layer, 0, j)),
        ],
        out_specs=pl.BlockSpec((IN_TM, IN_TN), lambda i, j: (i, j)),
        out_shape=jax.ShapeDtypeStruct((n_tok, P_PAD), BF16),
        scratch_shapes=[pltpu.VMEM((IN_TM, d), BF16)],
        compiler_params=_params("arbitrary", "arbitrary"),
        name="inproj",
    )(xs, mod, g_pre, w_in)


def _gmlp_kernel(u_ref, v_ref, lng_ref, lnb_ref, ws_ref, bs_ref, o_ref):
    def body(r):
        rows = pl.ds(r, CHUNK)
        for h in range(A_HEADS):
            cols = slice(h * A_HEAD_DIM, (h + 1) * A_HEAD_DIM)
            v = _gelu_tanh(v_ref[rows, cols].astype(F32))
            mu = jnp.mean(v, axis=-1, keepdims=True)
            vc = v - mu
            var = jnp.mean(vc * vc, axis=-1, keepdims=True)
            vn = vc * lax.rsqrt(var + EPS) * lng_ref[:, cols] + lnb_ref[:, cols]
            mixed = jnp.dot(ws_ref[h], vn.astype(BF16), preferred_element_type=F32)
            mixed = mixed + bs_ref[:, h:h + 1]
            u = _gelu_tanh(u_ref[rows, cols].astype(F32))
            o_ref[rows, cols] = (u * mixed).astype(BF16)
    _row_loop(GMLP_ROWS, body)


def _gmlp_call(p, ln_g, ln_b, w_s, b_s, layer):
    n_tok = p.shape[0]
    return pl.pallas_call(
        _gmlp_kernel,
        grid=(n_tok // GMLP_ROWS,),
        in_specs=[
            pl.BlockSpec((GMLP_ROWS, A_WIDTH), lambda i: (i, COL_U // A_WIDTH)),
            pl.BlockSpec((GMLP_ROWS, A_WIDTH), lambda i: (i, COL_V // A_WIDTH)),
            pl.BlockSpec((None, 1, A_WIDTH), lambda i: (layer, 0, 0)),
            pl.BlockSpec((None, 1, A_WIDTH), lambda i: (layer, 0, 0)),
            pl.BlockSpec((None, A_HEADS, CHUNK, CHUNK), lambda i: (layer, 0, 0, 0)),
            pl.BlockSpec((None, CHUNK, A_HEADS), lambda i: (layer, 0, 0)),
        ],
        out_specs=pl.BlockSpec((GMLP_ROWS, A_WIDTH), lambda i: (i, 0)),
        out_shape=jax.ShapeDtypeStruct((n_tok, A_WIDTH), BF16),
        compiler_params=_params("arbitrary"),
        name="gmlp",
    )(p, p, ln_g, ln_b, w_s, b_s)


def _log_sigmoid(z):
    return -(jnp.maximum(-z, 0.0) + jnp.log1p(jnp.exp(-jnp.abs(z))))


def _cumsum_rows(tri, la):
    h1 = la.astype(BF16)
    r1 = la - h1.astype(F32)
    h2 = r1.astype(BF16)
    h3 = (r1 - h2.astype(F32)).astype(BF16)
    dot = functools.partial(jnp.dot, preferred_element_type=F32)
    return dot(tri, h1) + dot(tri, h2) + dot(tri, h3)


def _gla_direction(q_ref, k_ref, v_ref, d_ref, wd_ref, bd_ref, s_ref, o_ref, reverse):
    row = lax.broadcasted_iota(jnp.int32, (CHUNK, CHUNK), 0)
    col = lax.broadcasted_iota(jnp.int32, (CHUNK, CHUNK), 1)
    causal = (col >= row) if reverse else (col <= row)
    eye = col == row
    tri = jnp.where(causal, 1.0, 0.0).astype(BF16)

    z = jnp.dot(d_ref[...], wd_ref[...], preferred_element_type=F32) + bd_ref[...]
    la = _log_sigmoid(z) * (1.0 / GATE_TAU)
    b = _cumsum_rows(tri, la)
    end = 0 if reverse else CHUNK - 1
    b_end = b[end:end + 1, :]
    b_mid = b[CHUNK // 2:CHUNK // 2 + 1, :]

    for h in range(B_HEADS):
        kc = slice(h * B_DK, (h + 1) * B_DK)
        vc = slice(h * B_DV, (h + 1) * B_DV)
        bh = b[:, kc]
        q = q_ref[:, kc].astype(F32) * (B_DK ** -0.5)
        k = k_ref[:, kc].astype(F32)
        v = v_ref[:, vc]
        q_in = (q * jnp.exp(bh)).astype(BF16)
        q_t = (q * jnp.exp(bh - b_mid[:, kc])).astype(BF16)
        k_t = (k * jnp.exp(b_mid[:, kc] - bh)).astype(BF16)
        k_d = k * jnp.exp(b_end[:, kc] - bh)
        scores = lax.dot_general(q_t, k_t, (((1,), (1,)), ((), ())),
                                 preferred_element_type=F32)
        scores = jnp.where(causal, scores, 0.0).astype(BF16)
        s = s_ref[h]
        o = jnp.dot(scores, v, preferred_element_type=F32)
        o = o + jnp.dot(q_in, s.astype(BF16), preferred_element_type=F32)
        o_ref[:, vc] = o
        decay = jnp.broadcast_to(jnp.exp(b_end[:, kc]), (B_DK, B_DK))
        decay_col = jnp.sum(jnp.where(eye, decay, 0.0), axis=-1, keepdims=True)
        s_ref[h] = decay_col * s + jnp.dot(k_d.T.astype(BF16), v, preferred_element_type=F32)


def _gla_kernel(qf_ref, kf_ref, vf_ref, df_ref, qb_ref, kb_ref, vb_ref, db_ref,
                wdf_ref, bdf_ref, wdb_ref, bdb_ref, of_ref, ob_ref, s_ref):
    @pl.when(pl.program_id(1) == 0)
    def _():
        s_ref[...] = jnp.zeros_like(s_ref)

    _gla_direction(qf_ref, kf_ref, vf_ref, df_ref, wdf_ref, bdf_ref, s_ref.at[0], of_ref, False)
    _gla_direction(qb_ref, kb_ref, vb_ref, db_ref, wdb_ref, bdb_ref, s_ref.at[1], ob_ref, True)


def _gla_call(p, wd_f, bd_f, wd_b, bd_b, layer, batch, n_x, n_c):
    n_tok = p.shape[0]

    def fwd(b, s):
        return jnp.where(s < n_c, batch * n_x + b * n_c + s, b * n_x + (s - n_c))

    def bwd(b, s):
        return jnp.where(s < n_c, batch * n_x + b * n_c + (n_c - 1 - s),
                         b * n_x + (n_x - 1 - (s - n_c)))

    def chunk_specs(order):
        return [
            pl.BlockSpec((CHUNK, B_KEY_W), lambda b, s: (order(b, s), COL_Q // B_KEY_W)),
            pl.BlockSpec((CHUNK, B_KEY_W), lambda b, s: (order(b, s), COL_K // B_KEY_W)),
            pl.BlockSpec((CHUNK, B_WIDTH), lambda b, s: (order(b, s), COL_VV // B_WIDTH)),
            pl.BlockSpec((CHUNK, D_BLOCK), lambda b, s: (order(b, s), COL_D // D_BLOCK)),
        ]

    def weight_specs():
        return [
            pl.BlockSpec((None, D_BLOCK, B_KEY_W), lambda b, s: (layer, 0, 0)),
            pl.BlockSpec((None, 1, B_KEY_W), lambda b, s: (layer, 0, 0)),
        ]

    out = jax.ShapeDtypeStruct((n_tok, B_WIDTH), F32)
    return pl.pallas_call(
        _gla_kernel,
        grid=(batch, n_x + n_c),
        in_specs=chunk_specs(fwd) + chunk_specs(bwd) + weight_specs() + weight_specs(),
        out_specs=[
            pl.BlockSpec((CHUNK, B_WIDTH), lambda b, s: (fwd(b, s), 0)),
            pl.BlockSpec((CHUNK, B_WIDTH), lambda b, s: (bwd(b, s), 0)),
        ],
        out_shape=[out, out],
        scratch_shapes=[pltpu.VMEM((2, B_HEADS, B_DK, B_DV), F32)],
        compiler_params=_params("arbitrary", "arbitrary"),
        name="gla",
    )(p, p, p, p, p, p, p, p, wd_f, bd_f, wd_b, bd_b)


def _mixout_kernel(a_ref, of_ref, ob_ref, g_ref, x_ref, mod_ref, og_ref, gpost_ref,
                   wo_ref, o_ref, gla_ref, y_ref):
    def head_norm(r):
        rows = pl.ds(r, ROW_BLK)
        for h in range(B_HEADS):
            cols = slice(h * B_DV, (h + 1) * B_DV)
            o = of_ref[rows, cols] + ob_ref[rows, cols]
            y = _rms(o) * og_ref[:, cols]
            gla_ref[rows, cols] = (y * _silu(g_ref[rows, cols].astype(F32))).astype(BF16)
    _row_loop(MIX_TM, head_norm)

    y_ref[...] = (
        jnp.dot(a_ref[...], wo_ref[0:A_WIDTH, :], preferred_element_type=F32)
        + jnp.dot(gla_ref[...], wo_ref[A_WIDTH:A_WIDTH + B_WIDTH, :], preferred_element_type=F32))

    def residual(r):
        rows = pl.ds(r, ROW_BLK)
        y = _rms(y_ref[rows, :]) * gpost_ref[...]
        o_ref[rows, :] = x_ref[rows, :] + mod_ref[2:3, :] * y
    _row_loop(MIX_TM, residual)


def _mixout_call(a, o_f, o_b, p, xs, mod, out_g, g_post, w_out, layer, mod_row):
    n_tok, d = xs.shape
    return pl.pallas_call(
        _mixout_kernel,
        grid=(n_tok // MIX_TM,),
        in_specs=[
            pl.BlockSpec((MIX_TM, A_WIDTH), lambda i: (i, 0)),
            pl.BlockSpec((MIX_TM, B_WIDTH), lambda i: (i, 0)),
            pl.BlockSpec((MIX_TM, B_WIDTH), lambda i: (i, 0)),
            pl.BlockSpec((MIX_TM, B_WIDTH), lambda i: (i, COL_G // B_WIDTH)),
            pl.BlockSpec((MIX_TM, d), lambda i: (i, 0)),
            pl.BlockSpec((None, None, 6, d), lambda i: (layer, mod_row(i, MIX_TM), 0, 0)),
            pl.BlockSpec((None, 1, B_WIDTH), lambda i: (layer, 0, 0)),
            pl.BlockSpec((None, 1, d), lambda i: (layer, 0, 0)),
            pl.BlockSpec((None, A_WIDTH + B_WIDTH, d), lambda i: (layer, 0, 0)),
        ],
        out_specs=pl.BlockSpec((MIX_TM, d), lambda i: (i, 0)),
        out_shape=jax.ShapeDtypeStruct((n_tok, d), F32),
        scratch_shapes=[pltpu.VMEM((MIX_TM, B_WIDTH), BF16), pltpu.VMEM((MIX_TM, d), F32)],
        input_output_aliases={4: 0},
        compiler_params=_params("arbitrary"),
        name="mixout",
    )(a, o_f, o_b, p, xs, mod, out_g, g_post, w_out)


def _ffn_kernel(x_ref, mod_ref, gpre_ref, gpost_ref, wa_ref, wb_ref, wo_ref, o_ref,
                h_ref, act_ref, y_ref):
    j = pl.program_id(1)

    @pl.when(j == 0)
    def _():
        gain = gpre_ref[...] * (1.0 + mod_ref[4:5, :])

        def body(r):
            rows = pl.ds(r, ROW_BLK)
            h = _rms(x_ref[rows, :]) * gain + mod_ref[3:4, :]
            h_ref[rows, :] = h.astype(BF16)
        _row_loop(FFN_TM, body)

    @pl.when(j < FFN_NH)
    def _():
        h = h_ref[...]
        a = jnp.dot(h, wa_ref[...], preferred_element_type=F32)
        b = jnp.dot(h, wb_ref[...], preferred_element_type=F32)
        act_ref[j] = (_silu(a) * b).astype(BF16)

    @pl.when(j >= FFN_NH)
    def _():
        acc = jnp.dot(act_ref[0], wo_ref[0:FFN_TH, :], preferred_element_type=F32)
        for k in range(1, FFN_NH):
            acc += jnp.dot(act_ref[k], wo_ref[k * FFN_TH:(k + 1) * FFN_TH, :],
                           preferred_element_type=F32)
        y_ref[j - FFN_NH] = acc

    @pl.when(j == FFN_NH + FFN_NO - 1)
    def _():
        gain = gpost_ref[...] * mod_ref[5:6, :]

        def body(r):
            rows = pl.ds(r, ROW_BLK)
            ys = [y_ref[n, rows, :] for n in range(FFN_NO)]
            ss = sum(jnp.sum(y * y, axis=-1, keepdims=True) for y in ys)
            inv = lax.rsqrt(ss * (1.0 / D_MODEL) + EPS)
            for n in range(FFN_NO):
                cols = slice(n * FFN_TN, (n + 1) * FFN_TN)
                o_ref[rows, cols] = x_ref[rows, cols] + ys[n] * inv * gain[:, cols]
        _row_loop(FFN_TM, body)


def _ffn_call(xs, mod, g_pre, g_post, w_ffn_in, w_ffn_out, layer, mod_row):
    n_tok, d = xs.shape
    last_h = FFN_NH - 1
    return pl.pallas_call(
        _ffn_kernel,
        grid=(n_tok // FFN_TM, FFN_NH + FFN_NO),
        in_specs=[
            pl.BlockSpec((FFN_TM, d), lambda i, j: (i, 0)),
            pl.BlockSpec((None, None, 6, d), lambda i, j: (layer, mod_row(i, FFN_TM), 0, 0)),
            pl.BlockSpec((None, 1, d), lambda i, j: (layer, 0, 0)),
            pl.BlockSpec((None, 1, d), lambda i, j: (layer, 0, 0)),
            pl.BlockSpec((None, d, FFN_TH), lambda i, j: (layer, 0, jnp.minimum(j, last_h))),
            pl.BlockSpec((None, d, FFN_TH),
                         lambda i, j: (layer, 0, FFN_NH + jnp.minimum(j, last_h))),
            pl.BlockSpec((None, FFN_HIDDEN, FFN_TN),
                         lambda i, j: (layer, 0, jnp.maximum(j - FFN_NH, 0))),
        ],
        out_specs=pl.BlockSpec((FFN_TM, d), lambda i, j: (i, 0)),
        out_shape=jax.ShapeDtypeStruct((n_tok, d), F32),
        scratch_shapes=[pltpu.VMEM((FFN_TM, d), BF16),
                        pltpu.VMEM((FFN_NH, FFN_TM, FFN_TH), BF16),
                        pltpu.VMEM((FFN_NO, FFN_TM, FFN_TN), F32)],
        input_output_aliases={0: 0},
        compiler_params=_params("arbitrary", "arbitrary"),
        name="ffn",
    )(xs, mod, g_pre, g_post, w_ffn_in, w_ffn_in, w_ffn_out)


def kernel(x, c, ctx, c_ctx, w_mod, b_mod, g_pre_mix, g_post_mix, g_pre_ffn, g_post_ffn, w_in, gmlp_ln_g, gmlp_ln_b, gmlp_ws, gmlp_bs, gla_wd2_fwd, gla_bd_fwd, gla_wd2_bwd, gla_bd_bwd, gla_out_g, w_out, w_ffn_in, w_ffn_out):
    batch, seq, d = x.shape
    ctx_len = ctx.shape[1]
    n_layers = w_mod.shape[0]
    n_lat = batch * seq
    n_ctx = batch * ctx_len
    assert d == D_MODEL and batch + 1 <= MOD_ROWS
    assert seq % IN_TM == 0 and n_ctx % IN_TM == 0 and ctx_len % CHUNK == 0

    def mod_row(i, tm):
        return jnp.where(i < n_lat // tm, i // (seq // tm), batch)

    xs = jnp.concatenate([x.reshape(n_lat, d), ctx.reshape(n_ctx, d)], axis=0)

    act = jnp.zeros((MOD_ROWS, d), F32).at[:batch].set(c).at[batch].set(c_ctx)
    mod = _mod_call(act, w_mod, b_mod).reshape(n_layers, MOD_ROWS, 6, d)

    row = lambda t: t.reshape(n_layers, 1, t.shape[-1])
    w_in_b = jnp.pad(w_in, ((0, 0), (0, 0), (0, P_PAD - P_COLS))).astype(BF16)
    w_out_b = w_out.astype(BF16)
    w_ffn_in_b = w_ffn_in.astype(BF16)
    w_ffn_out_b = w_ffn_out.astype(BF16)
    w_s_b = gmlp_ws.astype(BF16)
    b_s_t = jnp.swapaxes(gmlp_bs, 1, 2)
    wd_f = jnp.pad(gla_wd2_fwd, ((0, 0), (0, D_BLOCK - GATE_RANK), (0, 0))).astype(BF16)
    wd_b = jnp.pad(gla_wd2_bwd, ((0, 0), (GATE_RANK, D_BLOCK - 2 * GATE_RANK), (0, 0))).astype(BF16)

    for l in range(n_layers):
        p = _inproj_call(xs, mod, row(g_pre_mix), w_in_b, l, mod_row)
        a = _gmlp_call(p, row(gmlp_ln_g), row(gmlp_ln_b), w_s_b, b_s_t, l)
        o_f, o_b = _gla_call(p, wd_f, row(gla_bd_fwd), wd_b, row(gla_bd_bwd), l,
                             batch, seq // CHUNK, ctx_len // CHUNK)
        xs = _mixout_call(a, o_f, o_b, p, xs, mod, row(gla_out_g), row(g_post_mix),
                          w_out_b, l, mod_row)
        xs = _ffn_call(xs, mod, row(g_pre_ffn), row(g_post_ffn), w_ffn_in_b, w_ffn_out_b,
                       l, mod_row)
    return xs[:n_lat].reshape(batch, seq, d)
```

```python
import functools

import jax
import jax.numpy as jnp
from jax import lax
from jax.experimental import pallas as pl
from jax.experimental.pallas import tpu as pltpu

F32 = jnp.float32
BF16 = jnp.bfloat16

D_MODEL = 2048
CHUNK = 128
A_WIDTH = 1024
A_HEADS = 8
A_HEAD_DIM = A_WIDTH // A_HEADS
B_WIDTH = 1024
B_HEADS = 4
B_DV = B_WIDTH // B_HEADS
B_DK = B_DV // 2
B_KEY_W = B_HEADS * B_DK
GATE_RANK = 16
GATE_TAU = 16.0
FFN_HIDDEN = 5632
EPS = 1e-6

COL_U = 0
COL_V = A_WIDTH
COL_Q = 2 * A_WIDTH
COL_K = COL_Q + B_KEY_W
COL_VV = COL_K + B_KEY_W
COL_G = COL_VV + B_WIDTH
COL_D = COL_G + B_WIDTH
P_COLS = COL_D + 2 * GATE_RANK
P_PAD = 5376
D_BLOCK = 128

IN_TM = 1024
IN_TN = 768
MIX_TM = 512
FFN_TM = 512
FFN_TH = 512
FFN_TN = 512
FFN_NH = FFN_HIDDEN // FFN_TH
FFN_NO = D_MODEL // FFN_TN
GMLP_ROWS = 512
MOD_TN = 1024
MOD_ROWS = 8
ROW_BLK = 128

VMEM_LIMIT = 56 * 1024 * 1024


def _params(*sem):
    return pltpu.CompilerParams(dimension_semantics=sem, vmem_limit_bytes=VMEM_LIMIT)


def _row_loop(n_rows, body):
    def step(i, carry):
        body(pl.multiple_of(i * ROW_BLK, ROW_BLK))
        return carry
    lax.fori_loop(0, n_rows // ROW_BLK, step, 0)


def _silu(x):
    return x * jax.nn.sigmoid(x)


def _gelu_tanh(x):
    k = 0.7978845608028654
    return x * (0.5 * (1.0 + jnp.tanh(k * (x + 0.044715 * (x * x * x)))))


def _rms(x):
    return x * lax.rsqrt(jnp.mean(x * x, axis=-1, keepdims=True) + EPS)


def _mod_kernel(act_ref, w_ref, b_ref, o_ref):
    a = _silu(act_ref[...])
    o_ref[...] = jnp.dot(a.astype(BF16), w_ref[...].astype(BF16),
                         preferred_element_type=F32) + b_ref[...]


def _mod_call(act, w_mod, b_mod):
    n_layers, d, n = w_mod.shape
    return pl.pallas_call(
        _mod_kernel,
        grid=(n_layers, n // MOD_TN),
        in_specs=[
            pl.BlockSpec((MOD_ROWS, d), lambda l, j: (0, 0)),
            pl.BlockSpec((None, d, MOD_TN), lambda l, j: (l, 0, j)),
            pl.BlockSpec((None, 1, MOD_TN), lambda l, j: (l, 0, j)),
        ],
        out_specs=pl.BlockSpec((None, MOD_ROWS, MOD_TN), lambda l, j: (l, 0, j)),
        out_shape=jax.ShapeDtypeStruct((n_layers, MOD_ROWS, n), F32),
        compiler_params=_params("arbitrary", "arbitrary"),
        name="mod",
    )(act, w_mod, b_mod.reshape(n_layers, 1, n))


def _inproj_kernel(x_ref, mod_ref, g_ref, w_ref, o_ref, h_ref):
    @pl.when(pl.program_id(1) == 0)
    def _():
        def body(r):
            rows = pl.ds(r, ROW_BLK)
            y = _rms(x_ref[rows, :]) * g_ref[...]
            h = y * (1.0 + mod_ref[1:2, :]) + mod_ref[0:1, :]
            h_ref[rows, :] = h.astype(BF16)
        _row_loop(IN_TM, body)

    o_ref[...] = jnp.dot(h_ref[...], w_ref[...], preferred_element_type=F32).astype(BF16)


def _inproj_call(xs, mod, g_pre, w_in, layer, mod_row):
    n_tok, d = xs.shape
    return pl.pallas_call(
        _inproj_kernel,
        grid=(n_tok // IN_TM, P_PAD // IN_TN),
        in_specs=[
            pl.BlockSpec((IN_TM, d), lambda i, j: (i, 0)),
            pl.BlockSpec((None, None, 6, d), lambda i, j: (layer, mod_row(i, IN_TM), 0, 0)),
            pl.BlockSpec((None, 1, d), lambda i, j: (layer, 0, 0)),
            pl.BlockSpec((None, None, d, IN_TN), lambda i, j: (layer, j, 0, 0)),
        ],
        out_specs=pl.BlockSpec((IN_TM, IN_TN), lambda i, j: (i, j)),
        out_shape=jax.ShapeDtypeStruct((n_tok, P_PAD), BF16),
        scratch_shapes=[pltpu.VMEM((IN_TM, d), BF16)],
        compiler_params=_params("arbitrary", "arbitrary"),
        name="inproj",
    )(xs, mod, g_pre, w_in)


def _gmlp_kernel(u_ref, v_ref, lng_ref, lnb_ref, ws_ref, bs_ref, o_ref):
    def body(r):
        rows = pl.ds(r, CHUNK)
        for h in range(A_HEADS):
            cols = slice(h * A_HEAD_DIM, (h + 1) * A_HEAD_DIM)
            v = _gelu_tanh(v_ref[rows, cols].astype(F32))
            mu = jnp.mean(v, axis=-1, keepdims=True)
            vc = v - mu
            var = jnp.mean(vc * vc, axis=-1, keepdims=True)
            vn = vc * lax.rsqrt(var + EPS) * lng_ref[:, cols] + lnb_ref[:, cols]
            mixed = jnp.dot(ws_ref[h], vn.astype(BF16), preferred_element_type=F32)
            mixed = mixed + bs_ref[:, h:h + 1]
            u = _gelu_tanh(u_ref[rows, cols].astype(F32))
            o_ref[rows, cols] = (u * mixed).astype(BF16)
    _row_loop(GMLP_ROWS, body)


def _gmlp_call(p, ln_g, ln_b, w_s, b_s, layer, n_rows):
    return pl.pallas_call(
        _gmlp_kernel,
        grid=(n_rows // GMLP_ROWS,),
        in_specs=[
            pl.BlockSpec((GMLP_ROWS, A_WIDTH), lambda i: (i, COL_U // A_WIDTH)),
            pl.BlockSpec((GMLP_ROWS, A_WIDTH), lambda i: (i, COL_V // A_WIDTH)),
            pl.BlockSpec((None, 1, A_WIDTH), lambda i: (layer, 0, 0)),
            pl.BlockSpec((None, 1, A_WIDTH), lambda i: (layer, 0, 0)),
            pl.BlockSpec((None, A_HEADS, CHUNK, CHUNK), lambda i: (layer, 0, 0, 0)),
            pl.BlockSpec((None, CHUNK, A_HEADS), lambda i: (layer, 0, 0)),
        ],
        out_specs=pl.BlockSpec((GMLP_ROWS, A_WIDTH), lambda i: (i, 0)),
        out_shape=jax.ShapeDtypeStruct((n_rows, A_WIDTH), BF16),
        compiler_params=_params("arbitrary"),
        name="gmlp",
    )(p, p, ln_g, ln_b, w_s, b_s)


def _log_sigmoid(z):
    return -(jnp.maximum(-z, 0.0) + jnp.log1p(jnp.exp(-jnp.abs(z))))


def _cumsum_rows(tri, la):
    h1 = la.astype(BF16)
    r1 = la - h1.astype(F32)
    h2 = r1.astype(BF16)
    h3 = (r1 - h2.astype(F32)).astype(BF16)
    dot = functools.partial(jnp.dot, preferred_element_type=F32)
    return dot(tri, h1) + dot(tri, h2) + dot(tri, h3)


def _gla_direction(q_ref, k_ref, v_ref, d_ref, wd_ref, bd_ref, s_ref, o_ref, reverse):
    row = lax.broadcasted_iota(jnp.int32, (CHUNK, CHUNK), 0)
    col = lax.broadcasted_iota(jnp.int32, (CHUNK, CHUNK), 1)
    causal = (col >= row) if reverse else (col <= row)
    eye = col == row
    tri = jnp.where(causal, 1.0, 0.0).astype(BF16)

    z = jnp.dot(d_ref[...], wd_ref[...], preferred_element_type=F32) + bd_ref[...]
    la = _log_sigmoid(z) * (1.0 / GATE_TAU)
    b = _cumsum_rows(tri, la)
    end = 0 if reverse else CHUNK - 1
    b_end = b[end:end + 1, :]
    b_mid = b[CHUNK // 2:CHUNK // 2 + 1, :]

    for h in range(B_HEADS):
        kc = slice(h * B_DK, (h + 1) * B_DK)
        vc = slice(h * B_DV, (h + 1) * B_DV)
        bh = b[:, kc]
        q = q_ref[:, kc].astype(F32) * (B_DK ** -0.5)
        k = k_ref[:, kc].astype(F32)
        v = v_ref[:, vc]
        q_in = (q * jnp.exp(bh)).astype(BF16)
        q_t = (q * jnp.exp(bh - b_mid[:, kc])).astype(BF16)
        k_t = (k * jnp.exp(b_mid[:, kc] - bh)).astype(BF16)
        k_d = k * jnp.exp(b_end[:, kc] - bh)
        scores = lax.dot_general(q_t, k_t, (((1,), (1,)), ((), ())),
                                 preferred_element_type=F32)
        scores = jnp.where(causal, scores, 0.0).astype(BF16)
        s = s_ref[h]
        o = jnp.dot(scores, v, preferred_element_type=F32)
        o = o + jnp.dot(q_in, s.astype(BF16), preferred_element_type=F32)
        o_ref[:, vc] = o
        decay = jnp.broadcast_to(jnp.exp(b_end[:, kc]), (B_DK, B_DK))
        decay_col = jnp.sum(jnp.where(eye, decay, 0.0), axis=-1, keepdims=True)
        s_ref[h] = decay_col * s + jnp.dot(k_d.T.astype(BF16), v, preferred_element_type=F32)


def _gla_kernel(qf_ref, kf_ref, vf_ref, df_ref, qb_ref, kb_ref, vb_ref, db_ref,
                wdf_ref, bdf_ref, wdb_ref, bdb_ref, of_ref, ob_ref, s_ref):
    @pl.when(pl.program_id(1) == 0)
    def _():
        s_ref[...] = jnp.zeros_like(s_ref)

    _gla_direction(qf_ref, kf_ref, vf_ref, df_ref, wdf_ref, bdf_ref, s_ref.at[0], of_ref, False)
    _gla_direction(qb_ref, kb_ref, vb_ref, db_ref, wdb_ref, bdb_ref, s_ref.at[1], ob_ref, True)


def _gla_call(p, wd_f, bd_f, wd_b, bd_b, layer, batch, n_x, n_c):
    n_tok = p.shape[0]

    def fwd(b, s):
        return jnp.where(s < n_c, batch * n_x + b * n_c + s, b * n_x + (s - n_c))

    def bwd(b, s):
        return jnp.where(s < n_c, batch * n_x + b * n_c + (n_c - 1 - s),
                         b * n_x + (n_x - 1 - (s - n_c)))

    def chunk_specs(order):
        return [
            pl.BlockSpec((CHUNK, B_KEY_W), lambda b, s: (order(b, s), COL_Q // B_KEY_W)),
            pl.BlockSpec((CHUNK, B_KEY_W), lambda b, s: (order(b, s), COL_K // B_KEY_W)),
            pl.BlockSpec((CHUNK, B_WIDTH), lambda b, s: (order(b, s), COL_VV // B_WIDTH)),
            pl.BlockSpec((CHUNK, D_BLOCK), lambda b, s: (order(b, s), COL_D // D_BLOCK)),
        ]

    def weight_specs():
        return [
            pl.BlockSpec((None, D_BLOCK, B_KEY_W), lambda b, s: (layer, 0, 0)),
            pl.BlockSpec((None, 1, B_KEY_W), lambda b, s: (layer, 0, 0)),
        ]

    out = jax.ShapeDtypeStruct((n_tok, B_WIDTH), F32)
    return pl.pallas_call(
        _gla_kernel,
        grid=(batch, n_x + n_c),
        in_specs=chunk_specs(fwd) + chunk_specs(bwd) + weight_specs() + weight_specs(),
        out_specs=[
            pl.BlockSpec((CHUNK, B_WIDTH), lambda b, s: (fwd(b, s), 0)),
            pl.BlockSpec((CHUNK, B_WIDTH), lambda b, s: (bwd(b, s), 0)),
        ],
        out_shape=[out, out],
        scratch_shapes=[pltpu.VMEM((2, B_HEADS, B_DK, B_DV), F32)],
        compiler_params=_params("arbitrary", "arbitrary"),
        name="gla",
    )(p, p, p, p, p, p, p, p, wd_f, bd_f, wd_b, bd_b)


def _mixout_kernel(a_ref, of_ref, ob_ref, g_ref, x_ref, mod_ref, og_ref, gpost_ref,
                   wo_ref, o_ref, gla_ref, y_ref):
    def head_norm(r):
        rows = pl.ds(r, ROW_BLK)
        for h in range(B_HEADS):
            cols = slice(h * B_DV, (h + 1) * B_DV)
            o = of_ref[rows, cols] + ob_ref[rows, cols]
            y = _rms(o) * og_ref[:, cols]
            gla_ref[rows, cols] = (y * _silu(g_ref[rows, cols].astype(F32))).astype(BF16)
    _row_loop(MIX_TM, head_norm)

    y_ref[...] = (
        jnp.dot(a_ref[...], wo_ref[0:A_WIDTH, :], preferred_element_type=F32)
        + jnp.dot(gla_ref[...], wo_ref[A_WIDTH:A_WIDTH + B_WIDTH, :], preferred_element_type=F32))

    def residual(r):
        rows = pl.ds(r, ROW_BLK)
        y = _rms(y_ref[rows, :]) * gpost_ref[...]
        o_ref[rows, :] = x_ref[rows, :] + mod_ref[2:3, :] * y
    _row_loop(MIX_TM, residual)


def _mixout_call(a, o_f, o_b, p, xs, mod, out_g, g_post, w_out, layer, mod_row, n_rows):
    n_tok, d = xs.shape
    return pl.pallas_call(
        _mixout_kernel,
        grid=(n_rows // MIX_TM,),
        in_specs=[
            pl.BlockSpec((MIX_TM, A_WIDTH), lambda i: (i, 0)),
            pl.BlockSpec((MIX_TM, B_WIDTH), lambda i: (i, 0)),
            pl.BlockSpec((MIX_TM, B_WIDTH), lambda i: (i, 0)),
            pl.BlockSpec((MIX_TM, B_WIDTH), lambda i: (i, COL_G // B_WIDTH)),
            pl.BlockSpec((MIX_TM, d), lambda i: (i, 0)),
            pl.BlockSpec((None, None, 6, d), lambda i: (layer, mod_row(i, MIX_TM), 0, 0)),
            pl.BlockSpec((None, 1, B_WIDTH), lambda i: (layer, 0, 0)),
            pl.BlockSpec((None, 1, d), lambda i: (layer, 0, 0)),
            pl.BlockSpec((None, A_WIDTH + B_WIDTH, d), lambda i: (layer, 0, 0)),
        ],
        out_specs=pl.BlockSpec((MIX_TM, d), lambda i: (i, 0)),
        out_shape=jax.ShapeDtypeStruct((n_tok, d), F32),
        scratch_shapes=[pltpu.VMEM((MIX_TM, B_WIDTH), BF16), pltpu.VMEM((MIX_TM, d), F32)],
        input_output_aliases={4: 0},
        compiler_params=_params("arbitrary"),
        name="mixout",
    )(a, o_f, o_b, p, xs, mod, out_g, g_post, w_out)


def _ffn_kernel(x_ref, mod_ref, gpre_ref, gpost_ref, wa_ref, wb_ref, wo_ref, o_ref,
                h_ref, act_ref, y_ref):
    j = pl.program_id(1)

    @pl.when(j == 0)
    def _():
        gain = gpre_ref[...] * (1.0 + mod_ref[4:5, :])

        def body(r):
            rows = pl.ds(r, ROW_BLK)
            h = _rms(x_ref[rows, :]) * gain + mod_ref[3:4, :]
            h_ref[rows, :] = h.astype(BF16)
        _row_loop(FFN_TM, body)

    @pl.when(j < FFN_NH)
    def _():
        h = h_ref[...]
        a = jnp.dot(h, wa_ref[...], preferred_element_type=F32)
        b = jnp.dot(h, wb_ref[...], preferred_element_type=F32)
        act_ref[j] = (_silu(a) * b).astype(BF16)

    @pl.when(j >= FFN_NH)
    def _():
        acc = jnp.dot(act_ref[0], wo_ref[0:FFN_TH, :], preferred_element_type=F32)
        for k in range(1, FFN_NH):
            acc += jnp.dot(act_ref[k], wo_ref[k * FFN_TH:(k + 1) * FFN_TH, :],
                           preferred_element_type=F32)
        y_ref[j - FFN_NH] = acc

    @pl.when(j == FFN_NH + FFN_NO - 1)
    def _():
        gain = gpost_ref[...] * mod_ref[5:6, :]

        def body(r):
            rows = pl.ds(r, ROW_BLK)
            ys = [y_ref[n, rows, :] for n in range(FFN_NO)]
            ss = sum(jnp.sum(y * y, axis=-1, keepdims=True) for y in ys)
            inv = lax.rsqrt(ss * (1.0 / D_MODEL) + EPS)
            for n in range(FFN_NO):
                cols = slice(n * FFN_TN, (n + 1) * FFN_TN)
                o_ref[rows, cols] = x_ref[rows, cols] + ys[n] * inv * gain[:, cols]
        _row_loop(FFN_TM, body)


def _ffn_call(xs, mod, g_pre, g_post, w_ffn_in, w_ffn_out, layer, mod_row, n_rows):
    n_tok, d = xs.shape
    last_h = FFN_NH - 1
    return pl.pallas_call(
        _ffn_kernel,
        grid=(n_rows // FFN_TM, FFN_NH + FFN_NO),
        in_specs=[
            pl.BlockSpec((FFN_TM, d), lambda i, j: (i, 0)),
            pl.BlockSpec((None, None, 6, d), lambda i, j: (layer, mod_row(i, FFN_TM), 0, 0)),
            pl.BlockSpec((None, 1, d), lambda i, j: (layer, 0, 0)),
            pl.BlockSpec((None, 1, d), lambda i, j: (layer, 0, 0)),
            pl.BlockSpec((None, None, d, FFN_TH),
                         lambda i, j: (layer, jnp.minimum(j, last_h), 0, 0)),
            pl.BlockSpec((None, None, d, FFN_TH),
                         lambda i, j: (layer, FFN_NH + jnp.minimum(j, last_h), 0, 0)),
            pl.BlockSpec((None, None, FFN_HIDDEN, FFN_TN),
                         lambda i, j: (layer, jnp.maximum(j - FFN_NH, 0), 0, 0)),
        ],
        out_specs=pl.BlockSpec((FFN_TM, d), lambda i, j: (i, 0)),
        out_shape=jax.ShapeDtypeStruct((n_rows, d), F32),
        scratch_shapes=[pltpu.VMEM((FFN_TM, d), BF16),
                        pltpu.VMEM((FFN_NH, FFN_TM, FFN_TH), BF16),
                        pltpu.VMEM((FFN_NO, FFN_TM, FFN_TN), F32)],
        input_output_aliases={0: 0} if n_rows == n_tok else {},
        compiler_params=_params("arbitrary", "arbitrary"),
        name="ffn",
    )(xs, mod, g_pre, g_post, w_ffn_in, w_ffn_in, w_ffn_out)


def kernel(x, c, ctx, c_ctx, w_mod, b_mod, g_pre_mix, g_post_mix, g_pre_ffn, g_post_ffn, w_in, gmlp_ln_g, gmlp_ln_b, gmlp_ws, gmlp_bs, gla_wd2_fwd, gla_bd_fwd, gla_wd2_bwd, gla_bd_bwd, gla_out_g, w_out, w_ffn_in, w_ffn_out):
    batch, seq, d = x.shape
    ctx_len = ctx.shape[1]
    n_layers = w_mod.shape[0]
    n_lat = batch * seq
    n_ctx = batch * ctx_len
    assert d == D_MODEL and batch + 1 <= MOD_ROWS
    assert seq % IN_TM == 0 and n_ctx % IN_TM == 0 and ctx_len % CHUNK == 0

    def mod_row(i, tm):
        return jnp.where(i < n_lat // tm, i // (seq // tm), batch)

    xs = jnp.concatenate([x.reshape(n_lat, d), ctx.reshape(n_ctx, d)], axis=0)

    act = jnp.zeros((MOD_ROWS, d), F32).at[:batch].set(c).at[batch].set(c_ctx)
    mod = _mod_call(act, w_mod, b_mod).reshape(n_layers, MOD_ROWS, 6, d)

    row = lambda t: t.reshape(n_layers, 1, t.shape[-1])

    def col_blocks(w, tn):
        n_l, k, n = w.shape
        return jnp.swapaxes(w.astype(BF16).reshape(n_l, k, n // tn, tn), 1, 2)

    w_in_b = col_blocks(jnp.pad(w_in, ((0, 0), (0, 0), (0, P_PAD - P_COLS))), IN_TN)
    w_out_b = w_out.astype(BF16)
    w_ffn_in_b = col_blocks(w_ffn_in, FFN_TH)
    w_ffn_out_b = col_blocks(w_ffn_out, FFN_TN)
    w_s_b = gmlp_ws.astype(BF16)
    b_s_t = jnp.swapaxes(gmlp_bs, 1, 2)
    wd_f = jnp.pad(gla_wd2_fwd, ((0, 0), (0, D_BLOCK - GATE_RANK), (0, 0))).astype(BF16)
    wd_b = jnp.pad(gla_wd2_bwd, ((0, 0), (GATE_RANK, D_BLOCK - 2 * GATE_RANK), (0, 0))).astype(BF16)

    for l in range(n_layers):
        n_rows = n_lat if l == n_layers - 1 else n_lat + n_ctx
        p = _inproj_call(xs, mod, row(g_pre_mix), w_in_b, l, mod_row)
        a = _gmlp_call(p, row(gmlp_ln_g), row(gmlp_ln_b), w_s_b, b_s_t, l, n_rows)
        o_f, o_b = _gla_call(p, wd_f, row(gla_bd_fwd), wd_b, row(gla_bd_bwd), l,
                             batch, seq // CHUNK, ctx_len // CHUNK)
        xs = _mixout_call(a, o_f, o_b, p, xs, mod, row(gla_out_g), row(g_post_mix),
                          w_out_b, l, mod_row, n_rows)
        xs = _ffn_call(xs, mod, row(g_pre_ffn), row(g_post_ffn), w_ffn_in_b, w_ffn_out_b,
                       l, mod_row, n_rows)
    return xs.reshape(batch, seq, d)
```

```python
import functools

import jax
import jax.numpy as jnp
from jax import lax
from jax.experimental import pallas as pl
from jax.experimental.pallas import tpu as pltpu

F32 = jnp.float32
BF16 = jnp.bfloat16

D_MODEL = 2048
CHUNK = 128
A_WIDTH = 1024
A_HEADS = 8
A_HEAD_DIM = A_WIDTH // A_HEADS
B_WIDTH = 1024
B_HEADS = 4
B_DV = B_WIDTH // B_HEADS
B_DK = B_DV // 2
B_KEY_W = B_HEADS * B_DK
GATE_RANK = 16
GATE_TAU = 16.0
FFN_HIDDEN = 5632
EPS = 1e-6

COL_U = 0
COL_V = A_WIDTH
COL_Q = 2 * A_WIDTH
COL_K = COL_Q + B_KEY_W
COL_VV = COL_K + B_KEY_W
COL_G = COL_VV + B_WIDTH
COL_D = COL_G + B_WIDTH
P_COLS = COL_D + 2 * GATE_RANK
P_PAD = 5376
D_BLOCK = 128

IN_GROUPS = 2
IN_TN = 768
FFN_TH = 512
FFN_TN = 512
FFN_NH = FFN_HIDDEN // FFN_TH
FFN_NO = D_MODEL // FFN_TN
MOD_TN = 1024
MOD_ROWS = 8

VMEM_LIMIT = 56 * 1024 * 1024


def _params(*sem):
    return pltpu.CompilerParams(dimension_semantics=sem, vmem_limit_bytes=VMEM_LIMIT)


def _silu(x):
    return x * jax.nn.sigmoid(x)


def _gelu_tanh(x):
    k = 0.7978845608028654
    return x * (0.5 * (1.0 + jnp.tanh(k * (x + 0.044715 * (x * x * x)))))


def _rms(x):
    return x * lax.rsqrt(jnp.mean(x * x, axis=-1, keepdims=True) + EPS)


def _chunk_rows(rb):
    return pl.ds(pl.multiple_of(rb * CHUNK, CHUNK), CHUNK)


def _for_chunks(n, body):
    def step(rb, carry):
        body(rb)
        return carry
    lax.fori_loop(0, n, step, 0)


def _mod_row(group, rb, nb, n_ctx_groups):
    g = group + rb // nb
    return jnp.where(g < n_ctx_groups, nb, rb % nb)


def _load_canonical(ref):
    return lambda rb: ref[_chunk_rows(rb), :]


def _load_sample_major(ref, nb):
    return lambda rb: ref[rb % nb, _chunk_rows(rb // nb), :]


def _mod_kernel(act_ref, w_ref, b_ref, o_ref):
    a = _silu(act_ref[...])
    o_ref[...] = jnp.dot(a.astype(BF16), w_ref[...].astype(BF16),
                         preferred_element_type=F32) + b_ref[...]


def _mod_call(act, w_mod, b_mod):
    n_layers, d, n = w_mod.shape
    return pl.pallas_call(
        _mod_kernel,
        grid=(n_layers, n // MOD_TN),
        in_specs=[
            pl.BlockSpec((MOD_ROWS, d), lambda l, j: (0, 0)),
            pl.BlockSpec((None, d, MOD_TN), lambda l, j: (l, 0, j)),
            pl.BlockSpec((None, 1, MOD_TN), lambda l, j: (l, 0, j)),
        ],
        out_specs=pl.BlockSpec((None, MOD_ROWS, MOD_TN), lambda l, j: (l, 0, j)),
        out_shape=jax.ShapeDtypeStruct((n_layers, MOD_ROWS, n), F32),
        compiler_params=_params("arbitrary", "arbitrary"),
        name="mod",
    )(act, w_mod, b_mod.reshape(n_layers, 1, n))


def _x_specs(first, nb, groups, n_ctx_tiles, d, tile_of):
    if not first:
        return [pl.BlockSpec((groups * nb * CHUNK, d), lambda *ix: (tile_of(*ix), 0))]
    blk = (nb, groups * CHUNK, d)
    return [
        pl.BlockSpec(blk, lambda *ix: (0, jnp.maximum(tile_of(*ix) - n_ctx_tiles, 0), 0)),
        pl.BlockSpec(blk, lambda *ix: (0, jnp.minimum(tile_of(*ix), n_ctx_tiles - 1), 0)),
    ]


def _with_source(first, tile, n_ctx_tiles, nb, x_refs, fn):
    if not first:
        fn(_load_canonical(x_refs[0]))
        return

    @pl.when(tile >= n_ctx_tiles)
    def _():
        fn(_load_sample_major(x_refs[0], nb))

    @pl.when(tile < n_ctx_tiles)
    def _():
        fn(_load_sample_major(x_refs[1], nb))


def _inproj_kernel(*refs, first, nb, n_ctx_groups):
    n_x = 2 if first else 1
    x_refs = refs[:n_x]
    mod_ref, g_ref, w_ref, o_ref, h_ref = refs[n_x:]
    i = pl.program_id(0)

    @pl.when(pl.program_id(1) == 0)
    def _():
        def prologue(load):
            def body(rb):
                m = _mod_row(i * IN_GROUPS, rb, nb, n_ctx_groups)
                gain = g_ref[...] * (1.0 + mod_ref[m, 1:2, :])
                h = _rms(load(rb)) * gain + mod_ref[m, 0:1, :]
                h_ref[_chunk_rows(rb), :] = h.astype(BF16)
            _for_chunks(IN_GROUPS * nb, body)
        _with_source(first, i, n_ctx_groups // IN_GROUPS, nb, x_refs, prologue)

    o_ref[...] = jnp.dot(h_ref[...], w_ref[...], preferred_element_type=F32).astype(BF16)


def _inproj_call(x_args, mod, g_pre, w_in, layer, first, nb, n_groups, n_ctx_groups):
    d = D_MODEL
    tm = IN_GROUPS * nb * CHUNK
    kern = functools.partial(_inproj_kernel, first=first, nb=nb, n_ctx_groups=n_ctx_groups)
    return pl.pallas_call(
        kern,
        grid=(n_groups // IN_GROUPS, P_PAD // IN_TN),
        in_specs=_x_specs(first, nb, IN_GROUPS, n_ctx_groups // IN_GROUPS, d, lambda i, j: i) + [
            pl.BlockSpec((None, MOD_ROWS, 6, d), lambda i, j: (layer, 0, 0, 0)),
            pl.BlockSpec((None, 1, d), lambda i, j: (layer, 0, 0)),
            pl.BlockSpec((None, d, IN_TN), lambda i, j: (layer, 0, j)),
        ],
        out_specs=pl.BlockSpec((tm, IN_TN), lambda i, j: (i, j)),
        out_shape=jax.ShapeDtypeStruct((n_groups * nb * CHUNK, P_PAD), BF16),
        scratch_shapes=[pltpu.VMEM((tm, d), BF16)],
        compiler_params=_params("arbitrary", "arbitrary"),
        name="inproj",
    )(*x_args, mod, g_pre, w_in)


def _gmlp_kernel(u_ref, v_ref, lng_ref, lnb_ref, ws_ref, bs_ref, o_ref, *, nb):
    def body(rb):
        rows = _chunk_rows(rb)
        for h in range(A_HEADS):
            cols = slice(h * A_HEAD_DIM, (h + 1) * A_HEAD_DIM)
            v = _gelu_tanh(v_ref[rows, cols].astype(F32))
            mu = jnp.mean(v, axis=-1, keepdims=True)
            vc = v - mu
            var = jnp.mean(vc * vc, axis=-1, keepdims=True)
            vn = vc * lax.rsqrt(var + EPS) * lng_ref[:, cols] + lnb_ref[:, cols]
            mixed = jnp.dot(ws_ref[h], vn.astype(BF16), preferred_element_type=F32)
            mixed = mixed + bs_ref[:, h:h + 1]
            u = _gelu_tanh(u_ref[rows, cols].astype(F32))
            o_ref[rows, cols] = (u * mixed).astype(BF16)
    _for_chunks(nb, body)


def _gmlp_call(p, ln_g, ln_b, w_s, b_s, layer, nb, group0, n_groups):
    rows = nb * CHUNK
    return pl.pallas_call(
        functools.partial(_gmlp_kernel, nb=nb),
        grid=(n_groups - group0,),
        in_specs=[
            pl.BlockSpec((rows, A_WIDTH), lambda i: (i + group0, COL_U // A_WIDTH)),
            pl.BlockSpec((rows, A_WIDTH), lambda i: (i + group0, COL_V // A_WIDTH)),
            pl.BlockSpec((None, 1, A_WIDTH), lambda i: (layer, 0, 0)),
            pl.BlockSpec((None, 1, A_WIDTH), lambda i: (layer, 0, 0)),
            pl.BlockSpec((None, A_HEADS, CHUNK, CHUNK), lambda i: (layer, 0, 0, 0)),
            pl.BlockSpec((None, CHUNK, A_HEADS), lambda i: (layer, 0, 0)),
        ],
        out_specs=pl.BlockSpec((rows, A_WIDTH), lambda i: (i + group0, 0)),
        out_shape=jax.ShapeDtypeStruct((n_groups * rows, A_WIDTH), BF16),
        compiler_params=_params("arbitrary"),
        name="gmlp",
    )(p, p, ln_g, ln_b, w_s, b_s)


def _log_sigmoid(z):
    return -(jnp.maximum(-z, 0.0) + jnp.log1p(jnp.exp(-jnp.abs(z))))


def _cumsum_rows(tri, la):
    hi = la.astype(BF16)
    lo = (la - hi.astype(F32)).astype(BF16)
    return (jnp.dot(tri, hi, preferred_element_type=F32)
            + jnp.dot(tri, lo, preferred_element_type=F32))


def _gla_direction(q_ref, k_ref, v_ref, d_ref, wd_ref, bd_ref, s_ref, o_ref, reverse, nb):
    row = lax.broadcasted_iota(jnp.int32, (CHUNK, CHUNK), 0)
    col = lax.broadcasted_iota(jnp.int32, (CHUNK, CHUNK), 1)
    causal = (col >= row) if reverse else (col <= row)
    eye = col == row
    tri = jnp.where(causal, 1.0, 0.0).astype(BF16)
    end = 0 if reverse else CHUNK - 1
    mid = CHUNK // 2

    z = jnp.dot(d_ref[...], wd_ref[...], preferred_element_type=F32) + bd_ref[...]
    la_all = _log_sigmoid(z) * (1.0 / GATE_TAU)

    for b in range(nb):
        rows = slice(b * CHUNK, (b + 1) * CHUNK)
        cum = _cumsum_rows(tri, la_all[rows])
        c_end = cum[end:end + 1, :]
        c_mid = cum[mid:mid + 1, :]
        e_mid = jnp.exp(c_mid)
        e_end_mid = jnp.exp(c_end - c_mid)
        e_end = jnp.exp(c_end)
        for h in range(B_HEADS):
            kc = slice(h * B_DK, (h + 1) * B_DK)
            vc = slice(h * B_DV, (h + 1) * B_DV)
            rel = cum[:, kc] - c_mid[:, kc]
            q = q_ref[rows, kc].astype(F32) * (B_DK ** -0.5)
            k = k_ref[rows, kc].astype(F32)
            v = v_ref[rows, vc]
            q_t = q * jnp.exp(rel)
            k_t = k * jnp.exp(-rel)
            q_in = (q_t * e_mid[:, kc]).astype(BF16)
            k_d = k_t * e_end_mid[:, kc]
            scores = lax.dot_general(q_t.astype(BF16), k_t.astype(BF16),
                                     (((1,), (1,)), ((), ())), preferred_element_type=F32)
            scores = jnp.where(causal, scores, 0.0).astype(BF16)
            s = s_ref[b, h]
            o = jnp.dot(scores, v, preferred_element_type=F32)
            o = o + jnp.dot(q_in, s.astype(BF16), preferred_element_type=F32)
            o_ref[rows, vc] = o
            decay = jnp.broadcast_to(e_end[:, kc], (B_DK, B_DK))
            decay_col = jnp.sum(jnp.where(eye, decay, 0.0), axis=-1, keepdims=True)
            s_ref[b, h] = decay_col * s + jnp.dot(k_d.T.astype(BF16), v,
                                                  preferred_element_type=F32)


def _gla_kernel(qf_ref, kf_ref, vf_ref, df_ref, qb_ref, kb_ref, vb_ref, db_ref,
                wdf_ref, bdf_ref, wdb_ref, bdb_ref, of_ref, ob_ref, s_ref, *, nb):
    @pl.when(pl.program_id(0) == 0)
    def _():
        s_ref[...] = jnp.zeros_like(s_ref)

    _gla_direction(qf_ref, kf_ref, vf_ref, df_ref, wdf_ref, bdf_ref, s_ref.at[0], of_ref,
                   False, nb)
    _gla_direction(qb_ref, kb_ref, vb_ref, db_ref, wdb_ref, bdb_ref, s_ref.at[1], ob_ref,
                   True, nb)


def _gla_call(p, wd_f, bd_f, wd_b, bd_b, layer, nb, n_groups, n_ctx_groups):
    rows = nb * CHUNK

    def fwd(s):
        return s

    def bwd(s):
        return jnp.where(s < n_ctx_groups, n_ctx_groups - 1 - s, n_groups + n_ctx_groups - 1 - s)

    def chunk_specs(order):
        return [
            pl.BlockSpec((rows, B_KEY_W), lambda s: (order(s), COL_Q // B_KEY_W)),
            pl.BlockSpec((rows, B_KEY_W), lambda s: (order(s), COL_K // B_KEY_W)),
            pl.BlockSpec((rows, B_WIDTH), lambda s: (order(s), COL_VV // B_WIDTH)),
            pl.BlockSpec((rows, D_BLOCK), lambda s: (order(s), COL_D // D_BLOCK)),
        ]

    def weight_specs():
        return [
            pl.BlockSpec((None, D_BLOCK, B_KEY_W), lambda s: (layer, 0, 0)),
            pl.BlockSpec((None, 1, B_KEY_W), lambda s: (layer, 0, 0)),
        ]

    out = jax.ShapeDtypeStruct((n_groups * rows, B_WIDTH), F32)
    return pl.pallas_call(
        functools.partial(_gla_kernel, nb=nb),
        grid=(n_groups,),
        in_specs=chunk_specs(fwd) + chunk_specs(bwd) + weight_specs() + weight_specs(),
        out_specs=[
            pl.BlockSpec((rows, B_WIDTH), lambda s: (fwd(s), 0)),
            pl.BlockSpec((rows, B_WIDTH), lambda s: (bwd(s), 0)),
        ],
        out_shape=[out, out],
        scratch_shapes=[pltpu.VMEM((2, nb, B_HEADS, B_DK, B_DV), F32)],
        compiler_params=_params("arbitrary"),
        name="gla",
    )(p, p, p, p, p, p, p, p, wd_f, bd_f, wd_b, bd_b)


def _mixout_kernel(*refs, first, nb, group0, n_ctx_groups):
    n_x = 2 if first else 1
    a_ref, of_ref, ob_ref, g_ref = refs[:4]
    x_refs = refs[4:4 + n_x]
    mod_ref, og_ref, gpost_ref, wo_ref, o_ref, gla_ref, y_ref = refs[4 + n_x:]
    group = pl.program_id(0) + group0

    def head_norm(rb):
        rows = _chunk_rows(rb)
        for h in range(B_HEADS):
            cols = slice(h * B_DV, (h + 1) * B_DV)
            o = of_ref[rows, cols] + ob_ref[rows, cols]
            y = _rms(o) * og_ref[:, cols]
            gla_ref[rows, cols] = (y * _silu(g_ref[rows, cols].astype(F32))).astype(BF16)
    _for_chunks(nb, head_norm)

    y_ref[...] = (
        jnp.dot(a_ref[...], wo_ref[0:A_WIDTH, :], preferred_element_type=F32)
        + jnp.dot(gla_ref[...], wo_ref[A_WIDTH:A_WIDTH + B_WIDTH, :], preferred_element_type=F32))

    def epilogue(load):
        def residual(rb):
            m = _mod_row(group, rb, nb, n_ctx_groups)
            gain = gpost_ref[...] * mod_ref[m, 2:3, :]
            o_ref[_chunk_rows(rb), :] = load(rb) + _rms(y_ref[_chunk_rows(rb), :]) * gain
        _for_chunks(nb, residual)
    _with_source(first, group, n_ctx_groups, nb, x_refs, epilogue)


def _mixout_call(a, o_f, o_b, p, x_args, mod, out_g, g_post, w_out, layer, first, nb,
                 group0, n_groups, n_ctx_groups):
    d = D_MODEL
    rows = nb * CHUNK
    tile = lambda i: i + group0
    kern = functools.partial(_mixout_kernel, first=first, nb=nb, group0=group0,
                             n_ctx_groups=n_ctx_groups)
    n_x = len(x_args)
    return pl.pallas_call(
        kern,
        grid=(n_groups - group0,),
        in_specs=[
            pl.BlockSpec((rows, A_WIDTH), lambda i: (tile(i), 0)),
            pl.BlockSpec((rows, B_WIDTH), lambda i: (tile(i), 0)),
            pl.BlockSpec((rows, B_WIDTH), lambda i: (tile(i), 0)),
            pl.BlockSpec((rows, B_WIDTH), lambda i: (tile(i), COL_G // B_WIDTH)),
        ] + _x_specs(first, nb, 1, n_ctx_groups, d, tile) + [
            pl.BlockSpec((None, MOD_ROWS, 6, d), lambda i: (layer, 0, 0, 0)),
            pl.BlockSpec((None, 1, B_WIDTH), lambda i: (layer, 0, 0)),
            pl.BlockSpec((None, 1, d), lambda i: (layer, 0, 0)),
            pl.BlockSpec((None, A_WIDTH + B_WIDTH, d), lambda i: (layer, 0, 0)),
        ],
        out_specs=pl.BlockSpec((rows, d), lambda i: (tile(i), 0)),
        out_shape=jax.ShapeDtypeStruct((n_groups * rows, d), F32),
        scratch_shapes=[pltpu.VMEM((rows, B_WIDTH), BF16), pltpu.VMEM((rows, d), F32)],
        input_output_aliases={} if first else {4: 0},
        compiler_params=_params("arbitrary"),
        name="mixout",
    )(a, o_f, o_b, p, *x_args, mod, out_g, g_post, w_out)


def _ffn_kernel(x_ref, mod_ref, gpre_ref, gpost_ref, wa_ref, wb_ref, wo_ref, o_ref,
                h_ref, act_ref, y_ref, *, last, nb, group0, n_ctx_groups):
    j = pl.program_id(1)
    group = pl.program_id(0) + group0

    @pl.when(j == 0)
    def _():
        def body(rb):
            m = _mod_row(group, rb, nb, n_ctx_groups)
            gain = gpre_ref[...] * (1.0 + mod_ref[m, 4:5, :])
            h = _rms(x_ref[_chunk_rows(rb), :]) * gain + mod_ref[m, 3:4, :]
            h_ref[_chunk_rows(rb), :] = h.astype(BF16)
        _for_chunks(nb, body)

    @pl.when(j < FFN_NH)
    def _():
        h = h_ref[...]
        a = jnp.dot(h, wa_ref[...], preferred_element_type=F32)
        b = jnp.dot(h, wb_ref[...], preferred_element_type=F32)
        act_ref[j] = (_silu(a) * b).astype(BF16)

    @pl.when(j >= FFN_NH)
    def _():
        acc = jnp.dot(act_ref[0], wo_ref[0:FFN_TH, :], preferred_element_type=F32)
        for k in range(1, FFN_NH):
            acc += jnp.dot(act_ref[k], wo_ref[k * FFN_TH:(k + 1) * FFN_TH, :],
                           preferred_element_type=F32)
        y_ref[j - FFN_NH] = acc

    @pl.when(j == FFN_NH + FFN_NO - 1)
    def _():
        def body(rb):
            rows = _chunk_rows(rb)
            m = _mod_row(group, rb, nb, n_ctx_groups)
            gain = gpost_ref[...] * mod_ref[m, 5:6, :]
            ys = [y_ref[n, rows, :] for n in range(FFN_NO)]
            ss = sum(jnp.sum(y * y, axis=-1, keepdims=True) for y in ys)
            inv = lax.rsqrt(ss * (1.0 / D_MODEL) + EPS)
            for n in range(FFN_NO):
                cols = slice(n * FFN_TN, (n + 1) * FFN_TN)
                res = x_ref[rows, cols] + ys[n] * inv * gain[:, cols]
                if last:
                    o_ref[rb, :, cols] = res
                else:
                    o_ref[rows, cols] = res
        _for_chunks(nb, body)


def _ffn_call(xs, mod, g_pre, g_post, w_ffn_in, w_ffn_out, layer, last, nb, group0,
              n_groups, n_ctx_groups):
    d = D_MODEL
    rows = nb * CHUNK
    last_h = FFN_NH - 1
    tile = lambda i: i + group0
    if last:
        out_spec = pl.BlockSpec((nb, CHUNK, d), lambda i, j: (0, i, 0))
        out_shape = jax.ShapeDtypeStruct((nb, (n_groups - group0) * CHUNK, d), F32)
    else:
        out_spec = pl.BlockSpec((rows, d), lambda i, j: (tile(i), 0))
        out_shape = jax.ShapeDtypeStruct((n_groups * rows, d), F32)
    kern = functools.partial(_ffn_kernel, last=last, nb=nb, group0=group0,
                             n_ctx_groups=n_ctx_groups)
    return pl.pallas_call(
        kern,
        grid=(n_groups - group0, FFN_NH + FFN_NO),
        in_specs=[
            pl.BlockSpec((rows, d), lambda i, j: (tile(i), 0)),
            pl.BlockSpec((None, MOD_ROWS, 6, d), lambda i, j: (layer, 0, 0, 0)),
            pl.BlockSpec((None, 1, d), lambda i, j: (layer, 0, 0)),
            pl.BlockSpec((None, 1, d), lambda i, j: (layer, 0, 0)),
            pl.BlockSpec((None, d, FFN_TH), lambda i, j: (layer, 0, jnp.minimum(j, last_h))),
            pl.BlockSpec((None, d, FFN_TH),
                         lambda i, j: (layer, 0, FFN_NH + jnp.minimum(j, last_h))),
            pl.BlockSpec((None, FFN_HIDDEN, FFN_TN),
                         lambda i, j: (layer, 0, jnp.maximum(j - FFN_NH, 0))),
        ],
        out_specs=out_spec,
        out_shape=out_shape,
        scratch_shapes=[pltpu.VMEM((rows, d), BF16),
                        pltpu.VMEM((FFN_NH, rows, FFN_TH), BF16),
                        pltpu.VMEM((FFN_NO, rows, FFN_TN), F32)],
        input_output_aliases={} if last else {0: 0},
        compiler_params=_params("arbitrary", "arbitrary"),
        name="ffn",
    )(xs, mod, g_pre, g_post, w_ffn_in, w_ffn_in, w_ffn_out)


def kernel(x, c, ctx, c_ctx, w_mod, b_mod, g_pre_mix, g_post_mix, g_pre_ffn, g_post_ffn, w_in, gmlp_ln_g, gmlp_ln_b, gmlp_ws, gmlp_bs, gla_wd2_fwd, gla_bd_fwd, gla_wd2_bwd, gla_bd_bwd, gla_out_g, w_out, w_ffn_in, w_ffn_out):
    nb, seq, d = x.shape
    ctx_len = ctx.shape[1]
    n_layers = w_mod.shape[0]
    assert d == D_MODEL and nb + 1 <= MOD_ROWS
    assert seq % (IN_GROUPS * CHUNK) == 0 and ctx_len % (IN_GROUPS * CHUNK) == 0
    n_ctx_groups = ctx_len // CHUNK
    n_groups = n_ctx_groups + seq // CHUNK

    act = jnp.zeros((MOD_ROWS, d), F32).at[:nb].set(c).at[nb].set(c_ctx)
    mod = _mod_call(act, w_mod, b_mod).reshape(n_layers, MOD_ROWS, 6, d)

    row = lambda t: t.reshape(n_layers, 1, t.shape[-1])
    w_in_b = jnp.pad(w_in, ((0, 0), (0, 0), (0, P_PAD - P_COLS))).astype(BF16)
    w_out_b = w_out.astype(BF16)
    w_ffn_in_b = w_ffn_in.astype(BF16)
    w_ffn_out_b = w_ffn_out.astype(BF16)
    w_s_b = gmlp_ws.astype(BF16)
    b_s_t = jnp.swapaxes(gmlp_bs, 1, 2)
    wd_f = jnp.pad(gla_wd2_fwd, ((0, 0), (0, D_BLOCK - GATE_RANK), (0, 0))).astype(BF16)
    wd_b = jnp.pad(gla_wd2_bwd, ((0, 0), (GATE_RANK, D_BLOCK - 2 * GATE_RANK), (0, 0))).astype(BF16)

    x_args = (x, ctx)
    for l in range(n_layers):
        first = l == 0
        last = l == n_layers - 1
        group0 = n_ctx_groups if last else 0
        p = _inproj_call(x_args, mod, row(g_pre_mix), w_in_b, l, first, nb, n_groups,
                         n_ctx_groups)
        a = _gmlp_call(p, row(gmlp_ln_g), row(gmlp_ln_b), w_s_b, b_s_t, l, nb, group0, n_groups)
        o_f, o_b = _gla_call(p, wd_f, row(gla_bd_fwd), wd_b, row(gla_bd_bwd), l, nb,
                             n_groups, n_ctx_groups)
        xs = _mixout_call(a, o_f, o_b, p, x_args, mod, row(gla_out_g), row(g_post_mix),
                          w_out_b, l, first, nb, group0, n_groups, n_ctx_groups)
        xs = _ffn_call(xs, mod, row(g_pre_ffn), row(g_post_ffn), w_ffn_in_b, w_ffn_out_b,
                       l, last, nb, group0, n_groups, n_ctx_groups)
        x_args = (xs,)
    return xs
```

```python
import functools

import jax
import jax.numpy as jnp
from jax import lax
from jax.experimental import pallas as pl
from jax.experimental.pallas import tpu as pltpu

F32 = jnp.float32
BF16 = jnp.bfloat16

D_MODEL = 2048
CHUNK = 128
A_WIDTH = 1024
A_HEADS = 8
A_HEAD_DIM = A_WIDTH // A_HEADS
B_WIDTH = 1024
B_HEADS = 4
B_DV = B_WIDTH // B_HEADS
B_DK = B_DV // 2
B_KEY_W = B_HEADS * B_DK
GATE_RANK = 16
GATE_TAU = 16.0
FFN_HIDDEN = 5632
EPS = 1e-6

COL_U = 0
COL_V = A_WIDTH
COL_Q = 2 * A_WIDTH
COL_K = COL_Q + B_KEY_W
COL_VV = COL_K + B_KEY_W
COL_G = COL_VV + B_WIDTH
COL_D = COL_G + B_WIDTH
P_COLS = COL_D + 2 * GATE_RANK
P_PAD = 5376
D_BLOCK = 128

IN_GROUPS = 2
IN_TN = 768
FFN_TH = 512
FFN_TN = 512
FFN_NH = FFN_HIDDEN // FFN_TH
FFN_NO = D_MODEL // FFN_TN
MOD_TN = 1024
MOD_ROWS = 8

VMEM_LIMIT = 56 * 1024 * 1024


def _params(*sem):
    return pltpu.CompilerParams(dimension_semantics=sem, vmem_limit_bytes=VMEM_LIMIT)


def _silu(x):
    return x * jax.nn.sigmoid(x)


def _gelu_tanh(x):
    k = 0.7978845608028654
    return x * (0.5 * (1.0 + jnp.tanh(k * (x + 0.044715 * (x * x * x)))))


def _rms(x):
    return x * lax.rsqrt(jnp.mean(x * x, axis=-1, keepdims=True) + EPS)


def _chunk_rows(rb):
    return pl.ds(pl.multiple_of(rb * CHUNK, CHUNK), CHUNK)


def _for_chunks(n, body):
    def step(rb, carry):
        body(rb)
        return carry
    lax.fori_loop(0, n, step, 0)


def _mod_row(group, rb, nb, n_ctx_groups):
    g = group + rb // nb
    return jnp.where(g < n_ctx_groups, nb, rb % nb)


def _load_canonical(ref):
    return lambda rb: ref[_chunk_rows(rb), :]


def _load_sample_major(ref, nb):
    return lambda rb: ref[rb % nb, _chunk_rows(rb // nb), :]


def _mod_kernel(act_ref, w_ref, b_ref, o_ref):
    a = _silu(act_ref[...])
    o_ref[...] = jnp.dot(a.astype(BF16), w_ref[...].astype(BF16),
                         preferred_element_type=F32) + b_ref[...]


def _mod_call(act, w_mod, b_mod):
    n_layers, d, n = w_mod.shape
    return pl.pallas_call(
        _mod_kernel,
        grid=(n_layers, n // MOD_TN),
        in_specs=[
            pl.BlockSpec((MOD_ROWS, d), lambda l, j: (0, 0)),
            pl.BlockSpec((None, d, MOD_TN), lambda l, j: (l, 0, j)),
            pl.BlockSpec((None, 1, MOD_TN), lambda l, j: (l, 0, j)),
        ],
        out_specs=pl.BlockSpec((None, MOD_ROWS, MOD_TN), lambda l, j: (l, 0, j)),
        out_shape=jax.ShapeDtypeStruct((n_layers, MOD_ROWS, n), F32),
        compiler_params=_params("arbitrary", "arbitrary"),
        name="mod",
    )(act, w_mod, b_mod.reshape(n_layers, 1, n))


def _x_specs(first, nb, groups, n_ctx_tiles, d, tile_of):
    if not first:
        return [pl.BlockSpec((groups * nb * CHUNK, d), lambda *ix: (tile_of(*ix), 0))]
    blk = (nb, groups * CHUNK, d)
    return [
        pl.BlockSpec(blk, lambda *ix: (0, jnp.maximum(tile_of(*ix) - n_ctx_tiles, 0), 0)),
        pl.BlockSpec(blk, lambda *ix: (0, jnp.minimum(tile_of(*ix), n_ctx_tiles - 1), 0),
                     pipeline_mode=pl.Buffered(1)),
    ]


def _with_source(first, tile, n_ctx_tiles, nb, x_refs, fn):
    if not first:
        fn(_load_canonical(x_refs[0]))
        return

    @pl.when(tile >= n_ctx_tiles)
    def _():
        fn(_load_sample_major(x_refs[0], nb))

    @pl.when(tile < n_ctx_tiles)
    def _():
        fn(_load_sample_major(x_refs[1], nb))


def _inproj_kernel(*refs, first, nb, n_ctx_groups):
    n_x = 2 if first else 1
    x_refs = refs[:n_x]
    mod_ref, g_ref, w_ref, o_ref, h_ref = refs[n_x:]
    i = pl.program_id(0)

    @pl.when(pl.program_id(1) == 0)
    def _():
        def prologue(load):
            def body(rb):
                m = _mod_row(i * IN_GROUPS, rb, nb, n_ctx_groups)
                gain = g_ref[...] * (1.0 + mod_ref[m, 1:2, :])
                h = _rms(load(rb)) * gain + mod_ref[m, 0:1, :]
                h_ref[_chunk_rows(rb), :] = h.astype(BF16)
            _for_chunks(IN_GROUPS * nb, body)
        _with_source(first, i, n_ctx_groups // IN_GROUPS, nb, x_refs, prologue)

    o_ref[...] = jnp.dot(h_ref[...], w_ref[...], preferred_element_type=F32).astype(BF16)


def _inproj_call(x_args, mod, g_pre, w_in, layer, first, nb, n_groups, n_ctx_groups):
    d = D_MODEL
    tm = IN_GROUPS * nb * CHUNK
    kern = functools.partial(_inproj_kernel, first=first, nb=nb, n_ctx_groups=n_ctx_groups)
    return pl.pallas_call(
        kern,
        grid=(n_groups // IN_GROUPS, P_PAD // IN_TN),
        in_specs=_x_specs(first, nb, IN_GROUPS, n_ctx_groups // IN_GROUPS, d, lambda i, j: i) + [
            pl.BlockSpec((None, MOD_ROWS, 6, d), lambda i, j: (layer, 0, 0, 0)),
            pl.BlockSpec((None, 1, d), lambda i, j: (layer, 0, 0)),
            pl.BlockSpec((None, d, IN_TN), lambda i, j: (layer, 0, j)),
        ],
        out_specs=pl.BlockSpec((tm, IN_TN), lambda i, j: (i, j)),
        out_shape=jax.ShapeDtypeStruct((n_groups * nb * CHUNK, P_PAD), BF16),
        scratch_shapes=[pltpu.VMEM((tm, d), BF16)],
        compiler_params=_params("arbitrary", "arbitrary"),
        name="inproj",
    )(*x_args, mod, g_pre, w_in)


def _log_sigmoid(z):
    return -(jnp.maximum(-z, 0.0) + jnp.log1p(jnp.exp(-jnp.abs(z))))


def _cumsum_rows(tri, la):
    hi = la.astype(BF16)
    lo = (la - hi.astype(F32)).astype(BF16)
    return (jnp.dot(tri, hi, preferred_element_type=F32)
            + jnp.dot(tri, lo, preferred_element_type=F32))


def _gla_direction(q_ref, k_ref, v_ref, d_ref, wd_ref, bd_ref, s_ref, o_ref, reverse, nb):
    row = lax.broadcasted_iota(jnp.int32, (CHUNK, CHUNK), 0)
    col = lax.broadcasted_iota(jnp.int32, (CHUNK, CHUNK), 1)
    causal = (col >= row) if reverse else (col <= row)
    eye = col == row
    tri = jnp.where(causal, 1.0, 0.0).astype(BF16)
    end = 0 if reverse else CHUNK - 1
    mid = CHUNK // 2

    z = jnp.dot(d_ref[...], wd_ref[...], preferred_element_type=F32) + bd_ref[...]
    la_all = _log_sigmoid(z) * (1.0 / GATE_TAU)

    for b in range(nb):
        rows = slice(b * CHUNK, (b + 1) * CHUNK)
        cum = _cumsum_rows(tri, la_all[rows])
        c_end = cum[end:end + 1, :]
        c_mid = cum[mid:mid + 1, :]
        e_mid = jnp.exp(c_mid)
        e_end_mid = jnp.exp(c_end - c_mid)
        e_end = jnp.exp(c_end)
        for h in range(B_HEADS):
            kc = slice(h * B_DK, (h + 1) * B_DK)
            vc = slice(h * B_DV, (h + 1) * B_DV)
            rel = cum[:, kc] - c_mid[:, kc]
            q = q_ref[rows, kc].astype(F32) * (B_DK ** -0.5)
            k = k_ref[rows, kc].astype(F32)
            v = v_ref[rows, vc]
            q_t = q * jnp.exp(rel)
            k_t = k * jnp.exp(-rel)
            q_in = (q_t * e_mid[:, kc]).astype(BF16)
            k_d = k_t * e_end_mid[:, kc]
            scores = lax.dot_general(q_t.astype(BF16), k_t.astype(BF16),
                                     (((1,), (1,)), ((), ())), preferred_element_type=F32)
            scores = jnp.where(causal, scores, 0.0).astype(BF16)
            s = s_ref[b, h]
            o = jnp.dot(scores, v, preferred_element_type=F32)
            o = o + jnp.dot(q_in, s.astype(BF16), preferred_element_type=F32)
            o_ref[rows, vc] = o
            decay = jnp.broadcast_to(e_end[:, kc], (B_DK, B_DK))
            decay_col = jnp.sum(jnp.where(eye, decay, 0.0), axis=-1, keepdims=True)
            s_ref[b, h] = decay_col * s + jnp.dot(k_d.T.astype(BF16), v,
                                                  preferred_element_type=F32)


def _gla_kernel(qf_ref, kf_ref, vf_ref, df_ref, qb_ref, kb_ref, vb_ref, db_ref,
                wdf_ref, bdf_ref, wdb_ref, bdb_ref, of_ref, ob_ref, s_ref, *, nb):
    @pl.when(pl.program_id(0) == 0)
    def _():
        s_ref[...] = jnp.zeros_like(s_ref)

    _gla_direction(qf_ref, kf_ref, vf_ref, df_ref, wdf_ref, bdf_ref, s_ref.at[0], of_ref,
                   False, nb)
    _gla_direction(qb_ref, kb_ref, vb_ref, db_ref, wdb_ref, bdb_ref, s_ref.at[1], ob_ref,
                   True, nb)


def _gla_call(p, wd_f, bd_f, wd_b, bd_b, layer, nb, n_groups, n_ctx_groups):
    rows = nb * CHUNK

    def fwd(s):
        return s

    def bwd(s):
        return jnp.where(s < n_ctx_groups, n_ctx_groups - 1 - s, n_groups + n_ctx_groups - 1 - s)

    def chunk_specs(order):
        return [
            pl.BlockSpec((rows, B_KEY_W), lambda s: (order(s), COL_Q // B_KEY_W)),
            pl.BlockSpec((rows, B_KEY_W), lambda s: (order(s), COL_K // B_KEY_W)),
            pl.BlockSpec((rows, B_WIDTH), lambda s: (order(s), COL_VV // B_WIDTH)),
            pl.BlockSpec((rows, D_BLOCK), lambda s: (order(s), COL_D // D_BLOCK)),
        ]

    def weight_specs():
        return [
            pl.BlockSpec((None, D_BLOCK, B_KEY_W), lambda s: (layer, 0, 0)),
            pl.BlockSpec((None, 1, B_KEY_W), lambda s: (layer, 0, 0)),
        ]

    out = jax.ShapeDtypeStruct((n_groups * rows, B_WIDTH), F32)
    return pl.pallas_call(
        functools.partial(_gla_kernel, nb=nb),
        grid=(n_groups,),
        in_specs=chunk_specs(fwd) + chunk_specs(bwd) + weight_specs() + weight_specs(),
        out_specs=[
            pl.BlockSpec((rows, B_WIDTH), lambda s: (fwd(s), 0)),
            pl.BlockSpec((rows, B_WIDTH), lambda s: (bwd(s), 0)),
        ],
        out_shape=[out, out],
        scratch_shapes=[pltpu.VMEM((2, nb, B_HEADS, B_DK, B_DV), F32)],
        compiler_params=_params("arbitrary"),
        name="gla",
    )(p, p, p, p, p, p, p, p, wd_f, bd_f, wd_b, bd_b)


def _mix_prepare(u_ref, v_ref, of_ref, ob_ref, g_ref, lng_ref, lnb_ref, ws_ref, bs_ref,
                 og_ref, cat_ref, nb):
    for rb in range(nb):
        rows = slice(rb * CHUNK, (rb + 1) * CHUNK)
        for h in range(A_HEADS):
            cols = slice(h * A_HEAD_DIM, (h + 1) * A_HEAD_DIM)
            v = _gelu_tanh(v_ref[rows, cols].astype(F32))
            mu = jnp.mean(v, axis=-1, keepdims=True)
            vc = v - mu
            var = jnp.mean(vc * vc, axis=-1, keepdims=True)
            vn = vc * lax.rsqrt(var + EPS) * lng_ref[:, cols] + lnb_ref[:, cols]
            mixed = jnp.dot(ws_ref[h], vn.astype(BF16), preferred_element_type=F32)
            mixed = mixed + bs_ref[:, h:h + 1]
            u = _gelu_tanh(u_ref[rows, cols].astype(F32))
            cat_ref[rows, cols] = (u * mixed).astype(BF16)
        for h in range(B_HEADS):
            cols = slice(h * B_DV, (h + 1) * B_DV)
            o = of_ref[rows, cols] + ob_ref[rows, cols]
            y = _rms(o) * og_ref[:, cols]
            gated = y * _silu(g_ref[rows, cols].astype(F32))
            cat_ref[rows, A_WIDTH + h * B_DV:A_WIDTH + (h + 1) * B_DV] = gated.astype(BF16)


def _mix_kernel(*refs, first, nb, group0, n_ctx_groups):
    n_x = 2 if first else 1
    prep_refs = refs[:5]
    x_refs = refs[5:5 + n_x]
    (lng_ref, lnb_ref, ws_ref, bs_ref, mod_ref, og_ref, gpost_ref, wo_ref,
     o_ref, cat0_ref, cat1_ref, y_ref) = refs[5 + n_x:]
    s = pl.program_id(0)

    def prepare(cat_ref):
        _mix_prepare(*prep_refs, lng_ref, lnb_ref, ws_ref, bs_ref, og_ref, cat_ref, nb)

    def project(cat_new, cat_old):
        prepare(cat_new)
        y_ref[...] = jnp.dot(cat_old[...], wo_ref[...], preferred_element_type=F32)

    @pl.when(s == 0)
    def _():
        prepare(cat0_ref)

    @pl.when((s > 0) & (s % 2 == 1))
    def _():
        project(cat1_ref, cat0_ref)

    @pl.when((s > 0) & (s % 2 == 0))
    def _():
        project(cat0_ref, cat1_ref)

    @pl.when(s > 0)
    def _():
        group = s - 1 + group0

        def epilogue(load):
            def residual(rb):
                m = _mod_row(group, rb, nb, n_ctx_groups)
                gain = gpost_ref[...] * mod_ref[m, 2:3, :]
                o_ref[_chunk_rows(rb), :] = load(rb) + _rms(y_ref[_chunk_rows(rb), :]) * gain
            _for_chunks(nb, residual)
        _with_source(first, group, n_ctx_groups, nb, x_refs, epilogue)


def _mix_call(p, o_f, o_b, x_args, ln_g, ln_b, w_s, b_s, mod, out_g, g_post, w_out, layer,
              first, nb, group0, n_groups, n_ctx_groups):
    d = D_MODEL
    rows = nb * CHUNK
    n_tiles = n_groups - group0
    prep = lambda s: group0 + jnp.minimum(s, n_tiles - 1)
    post = lambda s: group0 + jnp.maximum(s - 1, 0)
    kern = functools.partial(_mix_kernel, first=first, nb=nb, group0=group0,
                             n_ctx_groups=n_ctx_groups)
    return pl.pallas_call(
        kern,
        grid=(n_tiles + 1,),
        in_specs=[
            pl.BlockSpec((rows, A_WIDTH), lambda s: (prep(s), COL_U // A_WIDTH)),
            pl.BlockSpec((rows, A_WIDTH), lambda s: (prep(s), COL_V // A_WIDTH)),
            pl.BlockSpec((rows, B_WIDTH), lambda s: (prep(s), 0)),
            pl.BlockSpec((rows, B_WIDTH), lambda s: (prep(s), 0)),
            pl.BlockSpec((rows, B_WIDTH), lambda s: (prep(s), COL_G // B_WIDTH)),
        ] + _x_specs(first, nb, 1, n_ctx_groups, d, post) + [
            pl.BlockSpec((None, 1, A_WIDTH), lambda s: (layer, 0, 0)),
            pl.BlockSpec((None, 1, A_WIDTH), lambda s: (layer, 0, 0)),
            pl.BlockSpec((None, A_HEADS, CHUNK, CHUNK), lambda s: (layer, 0, 0, 0)),
            pl.BlockSpec((None, CHUNK, A_HEADS), lambda s: (layer, 0, 0)),
            pl.BlockSpec((None, MOD_ROWS, 6, d), lambda s: (layer, 0, 0, 0)),
            pl.BlockSpec((None, 1, B_WIDTH), lambda s: (layer, 0, 0)),
            pl.BlockSpec((None, 1, d), lambda s: (layer, 0, 0)),
            pl.BlockSpec((None, A_WIDTH + B_WIDTH, d), lambda s: (layer, 0, 0),
                         pipeline_mode=pl.Buffered(1)),
        ],
        out_specs=pl.BlockSpec((rows, d), lambda s: (post(s), 0)),
        out_shape=jax.ShapeDtypeStruct((n_groups * rows, d), F32),
        scratch_shapes=[pltpu.VMEM((rows, A_WIDTH + B_WIDTH), BF16),
                        pltpu.VMEM((rows, A_WIDTH + B_WIDTH), BF16),
                        pltpu.VMEM((rows, d), F32)],
        input_output_aliases={} if first else {5: 0},
        compiler_params=_params("arbitrary"),
        name="mix",
    )(p, p, o_f, o_b, p, *x_args, ln_g, ln_b, w_s, b_s, mod, out_g, g_post, w_out)


def _ffn_kernel(x_ref, mod_ref, gpre_ref, gpost_ref, wa_ref, wb_ref, wo_ref, o_ref,
                h_ref, act_ref, y_ref, *, last, nb, group0, n_ctx_groups):
    j = pl.program_id(1)
    group = pl.program_id(0) + group0

    @pl.when(j == 0)
    def _():
        def body(rb):
            m = _mod_row(group, rb, nb, n_ctx_groups)
            gain = gpre_ref[...] * (1.0 + mod_ref[m, 4:5, :])
            h = _rms(x_ref[_chunk_rows(rb), :]) * gain + mod_ref[m, 3:4, :]
            h_ref[_chunk_rows(rb), :] = h.astype(BF16)
        _for_chunks(nb, body)

    @pl.when(j < FFN_NH)
    def _():
        h = h_ref[...]
        a = jnp.dot(h, wa_ref[...], preferred_element_type=F32)
        b = jnp.dot(h, wb_ref[...], preferred_element_type=F32)
        act_ref[j] = (_silu(a) * b).astype(BF16)

    @pl.when(j >= FFN_NH)
    def _():
        acc = jnp.dot(act_ref[0], wo_ref[0:FFN_TH, :], preferred_element_type=F32)
        for k in range(1, FFN_NH):
            acc += jnp.dot(act_ref[k], wo_ref[k * FFN_TH:(k + 1) * FFN_TH, :],
                           preferred_element_type=F32)
        y_ref[j - FFN_NH] = acc

    @pl.when(j == FFN_NH + FFN_NO - 1)
    def _():
        def body(rb):
            rows = _chunk_rows(rb)
            m = _mod_row(group, rb, nb, n_ctx_groups)
            gain = gpost_ref[...] * mod_ref[m, 5:6, :]
            ys = [y_ref[n, rows, :] for n in range(FFN_NO)]
            ss = sum(jnp.sum(y * y, axis=-1, keepdims=True) for y in ys)
            inv = lax.rsqrt(ss * (1.0 / D_MODEL) + EPS)
            for n in range(FFN_NO):
                cols = slice(n * FFN_TN, (n + 1) * FFN_TN)
                res = x_ref[rows, cols] + ys[n] * inv * gain[:, cols]
                if last:
                    o_ref[rb, :, cols] = res
                else:
                    o_ref[rows, cols] = res
        _for_chunks(nb, body)


def _ffn_call(xs, mod, g_pre, g_post, w_ffn_in, w_ffn_out, layer, last, nb, group0,
              n_groups, n_ctx_groups):
    d = D_MODEL
    rows = nb * CHUNK
    last_h = FFN_NH - 1
    tile = lambda i: i + group0
    if last:
        out_spec = pl.BlockSpec((nb, CHUNK, d), lambda i, j: (0, i, 0))
        out_shape = jax.ShapeDtypeStruct((nb, (n_groups - group0) * CHUNK, d), F32)
    else:
        out_spec = pl.BlockSpec((rows, d), lambda i, j: (tile(i), 0))
        out_shape = jax.ShapeDtypeStruct((n_groups * rows, d), F32)
    kern = functools.partial(_ffn_kernel, last=last, nb=nb, group0=group0,
                             n_ctx_groups=n_ctx_groups)
    return pl.pallas_call(
        kern,
        grid=(n_groups - group0, FFN_NH + FFN_NO),
        in_specs=[
            pl.BlockSpec((rows, d), lambda i, j: (tile(i), 0)),
            pl.BlockSpec((None, MOD_ROWS, 6, d), lambda i, j: (layer, 0, 0, 0)),
            pl.BlockSpec((None, 1, d), lambda i, j: (layer, 0, 0)),
            pl.BlockSpec((None, 1, d), lambda i, j: (layer, 0, 0)),
            pl.BlockSpec((None, d, FFN_TH), lambda i, j: (layer, 0, jnp.minimum(j, last_h))),
            pl.BlockSpec((None, d, FFN_TH),
                         lambda i, j: (layer, 0, FFN_NH + jnp.minimum(j, last_h))),
            pl.BlockSpec((None, FFN_HIDDEN, FFN_TN),
                         lambda i, j: (layer, 0, jnp.maximum(j - FFN_NH, 0))),
        ],
        out_specs=out_spec,
        out_shape=out_shape,
        scratch_shapes=[pltpu.VMEM((rows, d), BF16),
                        pltpu.VMEM((FFN_NH, rows, FFN_TH), BF16),
                        pltpu.VMEM((FFN_NO, rows, FFN_TN), F32)],
        input_output_aliases={} if last else {0: 0},
        compiler_params=_params("arbitrary", "arbitrary"),
        name="ffn",
    )(xs, mod, g_pre, g_post, w_ffn_in, w_ffn_in, w_ffn_out)


def kernel(x, c, ctx, c_ctx, w_mod, b_mod, g_pre_mix, g_post_mix, g_pre_ffn, g_post_ffn, w_in, gmlp_ln_g, gmlp_ln_b, gmlp_ws, gmlp_bs, gla_wd2_fwd, gla_bd_fwd, gla_wd2_bwd, gla_bd_bwd, gla_out_g, w_out, w_ffn_in, w_ffn_out):
    nb, seq, d = x.shape
    ctx_len = ctx.shape[1]
    n_layers = w_mod.shape[0]
    assert d == D_MODEL and nb + 1 <= MOD_ROWS
    assert seq % (IN_GROUPS * CHUNK) == 0 and ctx_len % (IN_GROUPS * CHUNK) == 0
    n_ctx_groups = ctx_len // CHUNK
    n_groups = n_ctx_groups + seq // CHUNK

    act = jnp.zeros((MOD_ROWS, d), F32).at[:nb].set(c).at[nb].set(c_ctx)
    mod = _mod_call(act, w_mod, b_mod).reshape(n_layers, MOD_ROWS, 6, d)

    row = lambda t: t.reshape(n_layers, 1, t.shape[-1])
    w_in_b = jnp.pad(w_in, ((0, 0), (0, 0), (0, P_PAD - P_COLS))).astype(BF16)
    w_out_b = w_out.astype(BF16)
    w_ffn_in_b = w_ffn_in.astype(BF16)
    w_ffn_out_b = w_ffn_out.astype(BF16)
    w_s_b = gmlp_ws.astype(BF16)
    b_s_t = jnp.swapaxes(gmlp_bs, 1, 2)
    wd_f = jnp.pad(gla_wd2_fwd, ((0, 0), (0, D_BLOCK - GATE_RANK), (0, 0))).astype(BF16)
    wd_b = jnp.pad(gla_wd2_bwd, ((0, 0), (GATE_RANK, D_BLOCK - 2 * GATE_RANK), (0, 0))).astype(BF16)

    x_args = (x, ctx)
    for l in range(n_layers):
        first = l == 0
        last = l == n_layers - 1
        group0 = n_ctx_groups if last else 0
        p = _inproj_call(x_args, mod, row(g_pre_mix), w_in_b, l, first, nb, n_groups,
                         n_ctx_groups)
        o_f, o_b = _gla_call(p, wd_f, row(gla_bd_fwd), wd_b, row(gla_bd_bwd), l, nb,
                             n_groups, n_ctx_groups)
        xs = _mix_call(p, o_f, o_b, x_args, row(gmlp_ln_g), row(gmlp_ln_b), w_s_b, b_s_t, mod,
                       row(gla_out_g), row(g_post_mix), w_out_b, l, first, nb, group0,
                       n_groups, n_ctx_groups)
        xs = _ffn_call(xs, mod, row(g_pre_ffn), row(g_post_ffn), w_ffn_in_b, w_ffn_out_b,
                       l, last, nb, group0, n_groups, n_ctx_groups)
        x_args = (xs,)
    return xs
```

```python
import functools

import jax
import jax.numpy as jnp
from jax import lax
from jax.experimental import pallas as pl
from jax.experimental.pallas import tpu as pltpu

F32 = jnp.float32
BF16 = jnp.bfloat16

D_MODEL = 2048
CHUNK = 128
A_WIDTH = 1024
A_HEADS = 8
A_HEAD_DIM = A_WIDTH // A_HEADS
B_WIDTH = 1024
B_HEADS = 4
B_DV = B_WIDTH // B_HEADS
B_DK = B_DV // 2
B_KEY_W = B_HEADS * B_DK
GATE_RANK = 16
GATE_TAU = 16.0
FFN_HIDDEN = 5632
EPS = 1e-6

COL_U = 0
COL_V = A_WIDTH
COL_Q = 2 * A_WIDTH
COL_K = COL_Q + B_KEY_W
COL_VV = COL_K + B_KEY_W
COL_G = COL_VV + B_WIDTH
COL_D = COL_G + B_WIDTH
P_COLS = COL_D + 2 * GATE_RANK
MXU_WIDTH = 256
P_PAD = -(-P_COLS // MXU_WIDTH) * MXU_WIDTH
D_BLOCK = 128

FFN_TH = 512
FFN_TN = 512
FFN_NH = FFN_HIDDEN // FFN_TH
FFN_NO = D_MODEL // FFN_TN
MOD_TN = 1024
MOD_ROWS = 8

VMEM_LIMIT = 56 * 1024 * 1024


def _params(*sem):
    return pltpu.CompilerParams(dimension_semantics=sem, vmem_limit_bytes=VMEM_LIMIT)


def _silu(x):
    return x * jax.nn.sigmoid(x)


def _gelu_tanh(x):
    k = 0.7978845608028654
    return x * (0.5 * (1.0 + jnp.tanh(k * (x + 0.044715 * (x * x * x)))))


def _rms(x):
    return x * lax.rsqrt(jnp.mean(x * x, axis=-1, keepdims=True) + EPS)


def _chunk_rows(rb):
    return pl.ds(pl.multiple_of(rb * CHUNK, CHUNK), CHUNK)


def _for_chunks(n, body):
    def step(rb, carry):
        body(rb)
        return carry
    lax.fori_loop(0, n, step, 0)


def _mod_row(group, rb, nb, n_ctx_groups):
    g = group + rb // nb
    return jnp.where(g < n_ctx_groups, nb, rb % nb)


def _load_canonical(ref):
    return lambda rb: ref[_chunk_rows(rb), :]


def _load_sample_major(ref, nb):
    return lambda rb: ref[rb % nb, _chunk_rows(rb // nb), :]


def _mod_kernel(act_ref, w_ref, b_ref, o_ref):
    a = _silu(act_ref[...])
    o_ref[...] = jnp.dot(a.astype(BF16), w_ref[...].astype(BF16),
                         preferred_element_type=F32) + b_ref[...]


def _mod_call(act, w_mod, b_mod):
    n_layers, d, n = w_mod.shape
    return pl.pallas_call(
        _mod_kernel,
        grid=(n_layers, n // MOD_TN),
        in_specs=[
            pl.BlockSpec((MOD_ROWS, d), lambda l, j: (0, 0)),
            pl.BlockSpec((None, d, MOD_TN), lambda l, j: (l, 0, j)),
            pl.BlockSpec((None, 1, MOD_TN), lambda l, j: (l, 0, j)),
        ],
        out_specs=pl.BlockSpec((None, MOD_ROWS, MOD_TN), lambda l, j: (l, 0, j)),
        out_shape=jax.ShapeDtypeStruct((n_layers, MOD_ROWS, n), F32),
        compiler_params=_params("arbitrary", "arbitrary"),
        name="mod",
    )(act, w_mod, b_mod.reshape(n_layers, 1, n))


def _x_specs(first, nb, groups, n_ctx_tiles, d, tile_of):
    if not first:
        return [pl.BlockSpec((groups * nb * CHUNK, d), lambda *ix: (tile_of(*ix), 0))]
    blk = (nb, groups * CHUNK, d)
    return [
        pl.BlockSpec(blk, lambda *ix: (0, jnp.maximum(tile_of(*ix) - n_ctx_tiles, 0), 0)),
        pl.BlockSpec(blk, lambda *ix: (0, jnp.minimum(tile_of(*ix), n_ctx_tiles - 1), 0),
                     pipeline_mode=pl.Buffered(1)),
    ]


def _with_source(first, tile, n_ctx_tiles, nb, x_refs, fn):
    if not first:
        fn(_load_canonical(x_refs[0]))
        return

    @pl.when(tile >= n_ctx_tiles)
    def _():
        fn(_load_sample_major(x_refs[0], nb))

    @pl.when(tile < n_ctx_tiles)
    def _():
        fn(_load_sample_major(x_refs[1], nb))


def _inproj_kernel(*refs, first, nb, n_groups, n_ctx_groups):
    n_x = 2 if first else 1
    x_refs = refs[:n_x]
    mod_ref, g_ref, w_ref, o_ref, h0_ref, h1_ref = refs[n_x:]
    s = pl.program_id(0)
    is_ctx = jnp.minimum(s, n_groups - 1) < n_ctx_groups

    def load(rb):
        if not first:
            return x_refs[0][rb * CHUNK:(rb + 1) * CHUNK, :]
        return jnp.where(is_ctx, x_refs[1][rb], x_refs[0][rb])

    def prepare(h_ref):
        for rb in range(nb):
            m = jnp.where(is_ctx, nb, rb)
            gain = g_ref[...] * (1.0 + mod_ref[m, 1:2, :])
            h = _rms(load(rb)) * gain + mod_ref[m, 0:1, :]
            h_ref[rb * CHUNK:(rb + 1) * CHUNK, :] = h.astype(BF16)

    def project(h_new, h_old):
        prepare(h_new)
        o_ref[...] = jnp.dot(h_old[...], w_ref[...], preferred_element_type=F32).astype(BF16)

    @pl.when(s == 0)
    def _():
        prepare(h0_ref)

    @pl.when((s > 0) & (s % 2 == 1))
    def _():
        project(h1_ref, h0_ref)

    @pl.when((s > 0) & (s % 2 == 0))
    def _():
        project(h0_ref, h1_ref)


def _inproj_call(x_args, mod, g_pre, w_in, layer, first, nb, n_groups, n_ctx_groups):
    d = D_MODEL
    rows = nb * CHUNK
    prep = lambda s: jnp.minimum(s, n_groups - 1)
    post = lambda s: jnp.maximum(s - 1, 0)
    kern = functools.partial(_inproj_kernel, first=first, nb=nb, n_groups=n_groups,
                             n_ctx_groups=n_ctx_groups)
    return pl.pallas_call(
        kern,
        grid=(n_groups + 1,),
        in_specs=_x_specs(first, nb, 1, n_ctx_groups, d, prep) + [
            pl.BlockSpec((None, MOD_ROWS, 6, d), lambda s: (layer, 0, 0, 0)),
            pl.BlockSpec((None, 1, d), lambda s: (layer, 0, 0)),
            pl.BlockSpec((None, d, P_PAD), lambda s: (layer, 0, 0),
                         pipeline_mode=pl.Buffered(1)),
        ],
        out_specs=pl.BlockSpec((rows, P_PAD), lambda s: (post(s), 0)),
        out_shape=jax.ShapeDtypeStruct((n_groups * rows, P_PAD), BF16),
        scratch_shapes=[pltpu.VMEM((rows, d), BF16), pltpu.VMEM((rows, d), BF16)],
        compiler_params=_params("arbitrary"),
        name="inproj",
    )(*x_args, mod, g_pre, w_in)


def _log_sigmoid(z):
    return -(jnp.maximum(-z, 0.0) + jnp.log1p(jnp.exp(-jnp.abs(z))))


def _cumsum_rows(tri, la):
    hi = la.astype(BF16)
    lo = (la - hi.astype(F32)).astype(BF16)
    return (jnp.dot(tri, hi, preferred_element_type=F32)
            + jnp.dot(tri, lo, preferred_element_type=F32))


def _gla_direction(q_ref, k_ref, v_ref, d_ref, wd_ref, bd_ref, s_ref, o_ref, reverse, nb):
    row = lax.broadcasted_iota(jnp.int32, (CHUNK, CHUNK), 0)
    col = lax.broadcasted_iota(jnp.int32, (CHUNK, CHUNK), 1)
    causal = (col >= row) if reverse else (col <= row)
    eye = col == row
    tri = jnp.where(causal, 1.0, 0.0).astype(BF16)
    end = 0 if reverse else CHUNK - 1
    mid = CHUNK // 2

    z = jnp.dot(d_ref[...], wd_ref[...], preferred_element_type=F32) + bd_ref[...]
    la_all = _log_sigmoid(z) * (1.0 / GATE_TAU)

    for b in range(nb):
        rows = slice(b * CHUNK, (b + 1) * CHUNK)
        cum = _cumsum_rows(tri, la_all[rows])
        c_end = cum[end:end + 1, :]
        c_mid = cum[mid:mid + 1, :]
        e_mid = jnp.exp(c_mid)
        e_end_mid = jnp.exp(c_end - c_mid)
        e_end = jnp.exp(c_end)
        for h in range(B_HEADS):
            kc = slice(h * B_DK, (h + 1) * B_DK)
            vc = slice(h * B_DV, (h + 1) * B_DV)
            rel = cum[:, kc] - c_mid[:, kc]
            q = q_ref[rows, kc].astype(F32) * (B_DK ** -0.5)
            k = k_ref[rows, kc].astype(F32)
            v = v_ref[rows, vc]
            q_t = q * jnp.exp(rel)
            k_t = k * jnp.exp(-rel)
            q_in = (q_t * e_mid[:, kc]).astype(BF16)
            k_d = k_t * e_end_mid[:, kc]
            scores = lax.dot_general(q_t.astype(BF16), k_t.astype(BF16),
                                     (((1,), (1,)), ((), ())), preferred_element_type=F32)
            scores = jnp.where(causal, scores, 0.0).astype(BF16)
            s = s_ref[b, h]
            o = jnp.dot(scores, v, preferred_element_type=F32)
            o = o + jnp.dot(q_in, s.astype(BF16), preferred_element_type=F32)
            o_ref[rows, vc] = o
            decay = jnp.broadcast_to(e_end[:, kc], (B_DK, B_DK))
            decay_col = jnp.sum(jnp.where(eye, decay, 0.0), axis=-1, keepdims=True)
            s_ref[b, h] = decay_col * s + jnp.dot(k_d.T.astype(BF16), v,
                                                  preferred_element_type=F32)


def _gla_kernel(qf_ref, kf_ref, vf_ref, df_ref, qb_ref, kb_ref, vb_ref, db_ref,
                wdf_ref, bdf_ref, wdb_ref, bdb_ref, of_ref, ob_ref, s_ref, *, nb):
    @pl.when(pl.program_id(0) == 0)
    def _():
        s_ref[...] = jnp.zeros_like(s_ref)

    _gla_direction(qf_ref, kf_ref, vf_ref, df_ref, wdf_ref, bdf_ref, s_ref.at[0], of_ref,
                   False, nb)
    _gla_direction(qb_ref, kb_ref, vb_ref, db_ref, wdb_ref, bdb_ref, s_ref.at[1], ob_ref,
                   True, nb)


def _gla_call(p, wd_f, bd_f, wd_b, bd_b, layer, nb, n_groups, n_ctx_groups):
    rows = nb * CHUNK

    def fwd(s):
        return s

    def bwd(s):
        return jnp.where(s < n_ctx_groups, n_ctx_groups - 1 - s, n_groups + n_ctx_groups - 1 - s)

    def chunk_specs(order):
        return [
            pl.BlockSpec((rows, B_KEY_W), lambda s: (order(s), COL_Q // B_KEY_W)),
            pl.BlockSpec((rows, B_KEY_W), lambda s: (order(s), COL_K // B_KEY_W)),
            pl.BlockSpec((rows, B_WIDTH), lambda s: (order(s), COL_VV // B_WIDTH)),
            pl.BlockSpec((rows, D_BLOCK), lambda s: (order(s), COL_D // D_BLOCK)),
        ]

    def weight_specs():
        return [
            pl.BlockSpec((None, D_BLOCK, B_KEY_W), lambda s: (layer, 0, 0)),
            pl.BlockSpec((None, 1, B_KEY_W), lambda s: (layer, 0, 0)),
        ]

    out = jax.ShapeDtypeStruct((n_groups * rows, B_WIDTH), F32)
    return pl.pallas_call(
        functools.partial(_gla_kernel, nb=nb),
        grid=(n_groups,),
        in_specs=chunk_specs(fwd) + chunk_specs(bwd) + weight_specs() + weight_specs(),
        out_specs=[
            pl.BlockSpec((rows, B_WIDTH), lambda s: (fwd(s), 0)),
            pl.BlockSpec((rows, B_WIDTH), lambda s: (bwd(s), 0)),
        ],
        out_shape=[out, out],
        scratch_shapes=[pltpu.VMEM((2, nb, B_HEADS, B_DK, B_DV), F32)],
        compiler_params=_params("arbitrary"),
        name="gla",
    )(p, p, p, p, p, p, p, p, wd_f, bd_f, wd_b, bd_b)


def _mix_prepare(u_ref, v_ref, of_ref, ob_ref, g_ref, lng_ref, lnb_ref, ws_ref, bs_ref,
                 og_ref, cat_ref, nb):
    for rb in range(nb):
        rows = slice(rb * CHUNK, (rb + 1) * CHUNK)
        for h in range(A_HEADS):
            cols = slice(h * A_HEAD_DIM, (h + 1) * A_HEAD_DIM)
            v = _gelu_tanh(v_ref[rows, cols].astype(F32))
            mu = jnp.mean(v, axis=-1, keepdims=True)
            vc = v - mu
            var = jnp.mean(vc * vc, axis=-1, keepdims=True)
            vn = vc * lax.rsqrt(var + EPS) * lng_ref[:, cols] + lnb_ref[:, cols]
            mixed = jnp.dot(ws_ref[h], vn.astype(BF16), preferred_element_type=F32)
            mixed = mixed + bs_ref[:, h:h + 1]
            u = _gelu_tanh(u_ref[rows, cols].astype(F32))
            cat_ref[rows, cols] = (u * mixed).astype(BF16)
        for h in range(B_HEADS):
            cols = slice(h * B_DV, (h + 1) * B_DV)
            o = of_ref[rows, cols] + ob_ref[rows, cols]
            y = _rms(o) * og_ref[:, cols]
            gated = y * _silu(g_ref[rows, cols].astype(F32))
            cat_ref[rows, A_WIDTH + h * B_DV:A_WIDTH + (h + 1) * B_DV] = gated.astype(BF16)


def _mix_kernel(*refs, first, nb, group0, n_ctx_groups):
    n_x = 2 if first else 1
    prep_refs = refs[:5]
    x_refs = refs[5:5 + n_x]
    (lng_ref, lnb_ref, ws_ref, bs_ref, mod_ref, og_ref, gpost_ref, wo_ref,
     o_ref, cat0_ref, cat1_ref, y_ref) = refs[5 + n_x:]
    s = pl.program_id(0)

    def prepare(cat_ref):
        _mix_prepare(*prep_refs, lng_ref, lnb_ref, ws_ref, bs_ref, og_ref, cat_ref, nb)

    def project(cat_new, cat_old):
        prepare(cat_new)
        y_ref[...] = jnp.dot(cat_old[...], wo_ref[...], preferred_element_type=F32)

    @pl.when(s == 0)
    def _():
        prepare(cat0_ref)

    @pl.when((s > 0) & (s % 2 == 1))
    def _():
        project(cat1_ref, cat0_ref)

    @pl.when((s > 0) & (s % 2 == 0))
    def _():
        project(cat0_ref, cat1_ref)

    @pl.when(s > 0)
    def _():
        group = s - 1 + group0

        def epilogue(load):
            def residual(rb):
                m = _mod_row(group, rb, nb, n_ctx_groups)
                gain = gpost_ref[...] * mod_ref[m, 2:3, :]
                o_ref[_chunk_rows(rb), :] = load(rb) + _rms(y_ref[_chunk_rows(rb), :]) * gain
            _for_chunks(nb, residual)
        _with_source(first, group, n_ctx_groups, nb, x_refs, epilogue)


def _mix_call(p, o_f, o_b, x_args, ln_g, ln_b, w_s, b_s, mod, out_g, g_post, w_out, layer,
              first, nb, group0, n_groups, n_ctx_groups):
    d = D_MODEL
    rows = nb * CHUNK
    n_tiles = n_groups - group0
    prep = lambda s: group0 + jnp.minimum(s, n_tiles - 1)
    post = lambda s: group0 + jnp.maximum(s - 1, 0)
    kern = functools.partial(_mix_kernel, first=first, nb=nb, group0=group0,
                             n_ctx_groups=n_ctx_groups)
    return pl.pallas_call(
        kern,
        grid=(n_tiles + 1,),
        in_specs=[
            pl.BlockSpec((rows, A_WIDTH), lambda s: (prep(s), COL_U // A_WIDTH)),
            pl.BlockSpec((rows, A_WIDTH), lambda s: (prep(s), COL_V // A_WIDTH)),
            pl.BlockSpec((rows, B_WIDTH), lambda s: (prep(s), 0)),
            pl.BlockSpec((rows, B_WIDTH), lambda s: (prep(s), 0)),
            pl.BlockSpec((rows, B_WIDTH), lambda s: (prep(s), COL_G // B_WIDTH)),
        ] + _x_specs(first, nb, 1, n_ctx_groups, d, post) + [
            pl.BlockSpec((None, 1, A_WIDTH), lambda s: (layer, 0, 0)),
            pl.BlockSpec((None, 1, A_WIDTH), lambda s: (layer, 0, 0)),
            pl.BlockSpec((None, A_HEADS, CHUNK, CHUNK), lambda s: (layer, 0, 0, 0)),
            pl.BlockSpec((None, CHUNK, A_HEADS), lambda s: (layer, 0, 0)),
            pl.BlockSpec((None, MOD_ROWS, 6, d), lambda s: (layer, 0, 0, 0)),
            pl.BlockSpec((None, 1, B_WIDTH), lambda s: (layer, 0, 0)),
            pl.BlockSpec((None, 1, d), lambda s: (layer, 0, 0)),
            pl.BlockSpec((None, A_WIDTH + B_WIDTH, d), lambda s: (layer, 0, 0),
                         pipeline_mode=pl.Buffered(1)),
        ],
        out_specs=pl.BlockSpec((rows, d), lambda s: (post(s), 0)),
        out_shape=jax.ShapeDtypeStruct((n_groups * rows, d), F32),
        scratch_shapes=[pltpu.VMEM((rows, A_WIDTH + B_WIDTH), BF16),
                        pltpu.VMEM((rows, A_WIDTH + B_WIDTH), BF16),
                        pltpu.VMEM((rows, d), F32)],
        input_output_aliases={} if first else {5: 0},
        compiler_params=_params("arbitrary"),
        name="mix",
    )(p, p, o_f, o_b, p, *x_args, ln_g, ln_b, w_s, b_s, mod, out_g, g_post, w_out)


def _ffn_kernel(x_ref, mod_ref, gpre_ref, gpost_ref, wa_ref, wb_ref, wo_ref, o_ref,
                h_ref, act_ref, y_ref, *, last, nb, group0, n_ctx_groups):
    j = pl.program_id(1)
    group = pl.program_id(0) + group0

    @pl.when(j == 0)
    def _():
        def body(rb):
            m = _mod_row(group, rb, nb, n_ctx_groups)
            gain = gpre_ref[...] * (1.0 + mod_ref[m, 4:5, :])
            h = _rms(x_ref[_chunk_rows(rb), :]) * gain + mod_ref[m, 3:4, :]
            h_ref[_chunk_rows(rb), :] = h.astype(BF16)
        _for_chunks(nb, body)

    @pl.when(j < FFN_NH)
    def _():
        h = h_ref[...]
        a = jnp.dot(h, wa_ref[...], preferred_element_type=F32)
        b = jnp.dot(h, wb_ref[...], preferred_element_type=F32)
        act_ref[j] = (_silu(a) * b).astype(BF16)

    @pl.when(j >= FFN_NH)
    def _():
        acc = jnp.dot(act_ref[0], wo_ref[0:FFN_TH, :], preferred_element_type=F32)
        for k in range(1, FFN_NH):
            acc += jnp.dot(act_ref[k], wo_ref[k * FFN_TH:(k + 1) * FFN_TH, :],
                           preferred_element_type=F32)
        y_ref[j - FFN_NH] = acc

    @pl.when(j == FFN_NH + FFN_NO - 1)
    def _():
        def body(rb):
            rows = _chunk_rows(rb)
            m = _mod_row(group, rb, nb, n_ctx_groups)
            gain = gpost_ref[...] * mod_ref[m, 5:6, :]
            ys = [y_ref[n, rows, :] for n in range(FFN_NO)]
            ss = sum(jnp.sum(y * y, axis=-1, keepdims=True) for y in ys)
            inv = lax.rsqrt(ss * (1.0 / D_MODEL) + EPS)
            for n in range(FFN_NO):
                cols = slice(n * FFN_TN, (n + 1) * FFN_TN)
                res = x_ref[rows, cols] + ys[n] * inv * gain[:, cols]
                if last:
                    o_ref[rb, :, cols] = res
                else:
                    o_ref[rows, cols] = res
        _for_chunks(nb, body)


def _ffn_call(xs, mod, g_pre, g_post, w_ffn_in, w_ffn_out, layer, last, nb, group0,
              n_groups, n_ctx_groups):
    d = D_MODEL
    rows = nb * CHUNK
    last_h = FFN_NH - 1
    tile = lambda i: i + group0
    if last:
        out_spec = pl.BlockSpec((nb, CHUNK, d), lambda i, j: (0, i, 0))
        out_shape = jax.ShapeDtypeStruct((nb, (n_groups - group0) * CHUNK, d), F32)
    else:
        out_spec = pl.BlockSpec((rows, d), lambda i, j: (tile(i), 0))
        out_shape = jax.ShapeDtypeStruct((n_groups * rows, d), F32)
    kern = functools.partial(_ffn_kernel, last=last, nb=nb, group0=group0,
                             n_ctx_groups=n_ctx_groups)
    return pl.pallas_call(
        kern,
        grid=(n_groups - group0, FFN_NH + FFN_NO),
        in_specs=[
            pl.BlockSpec((rows, d), lambda i, j: (tile(i), 0)),
            pl.BlockSpec((None, MOD_ROWS, 6, d), lambda i, j: (layer, 0, 0, 0)),
            pl.BlockSpec((None, 1, d), lambda i, j: (layer, 0, 0)),
            pl.BlockSpec((None, 1, d), lambda i, j: (layer, 0, 0)),
            pl.BlockSpec((None, d, FFN_TH), lambda i, j: (layer, 0, jnp.minimum(j, last_h))),
            pl.BlockSpec((None, d, FFN_TH),
                         lambda i, j: (layer, 0, FFN_NH + jnp.minimum(j, last_h))),
            pl.BlockSpec((None, FFN_HIDDEN, FFN_TN),
                         lambda i, j: (layer, 0, jnp.maximum(j - FFN_NH, 0))),
        ],
        out_specs=out_spec,
        out_shape=out_shape,
        scratch_shapes=[pltpu.VMEM((rows, d), BF16),
                        pltpu.VMEM((FFN_NH, rows, FFN_TH), BF16),
                        pltpu.VMEM((FFN_NO, rows, FFN_TN), F32)],
        input_output_aliases={} if last else {0: 0},
        compiler_params=_params("arbitrary", "arbitrary"),
        name="ffn",
    )(xs, mod, g_pre, g_post, w_ffn_in, w_ffn_in, w_ffn_out)


def kernel(x, c, ctx, c_ctx, w_mod, b_mod, g_pre_mix, g_post_mix, g_pre_ffn, g_post_ffn, w_in, gmlp_ln_g, gmlp_ln_b, gmlp_ws, gmlp_bs, gla_wd2_fwd, gla_bd_fwd, gla_wd2_bwd, gla_bd_bwd, gla_out_g, w_out, w_ffn_in, w_ffn_out):
    nb, seq, d = x.shape
    ctx_len = ctx.shape[1]
    n_layers = w_mod.shape[0]
    assert d == D_MODEL and nb + 1 <= MOD_ROWS
    assert seq % CHUNK == 0 and ctx_len % CHUNK == 0
    n_ctx_groups = ctx_len // CHUNK
    n_groups = n_ctx_groups + seq // CHUNK

    act = jnp.zeros((MOD_ROWS, d), F32).at[:nb].set(c).at[nb].set(c_ctx)
    mod = _mod_call(act, w_mod, b_mod).reshape(n_layers, MOD_ROWS, 6, d)

    row = lambda t: t.reshape(n_layers, 1, t.shape[-1])
    w_in_b = jnp.pad(w_in, ((0, 0), (0, 0), (0, P_PAD - P_COLS))).astype(BF16)
    w_out_b = w_out.astype(BF16)
    w_ffn_in_b = w_ffn_in.astype(BF16)
    w_ffn_out_b = w_ffn_out.astype(BF16)
    w_s_b = gmlp_ws.astype(BF16)
    b_s_t = jnp.swapaxes(gmlp_bs, 1, 2)
    wd_f = jnp.pad(gla_wd2_fwd, ((0, 0), (0, D_BLOCK - GATE_RANK), (0, 0))).astype(BF16)
    wd_b = jnp.pad(gla_wd2_bwd, ((0, 0), (GATE_RANK, D_BLOCK - 2 * GATE_RANK), (0, 0))).astype(BF16)

    x_args = (x, ctx)
    for l in range(n_layers):
        first = l == 0
        last = l == n_layers - 1
        group0 = n_ctx_groups if last else 0
        p = _inproj_call(x_args, mod, row(g_pre_mix), w_in_b, l, first, nb, n_groups,
                         n_ctx_groups)
        o_f, o_b = _gla_call(p, wd_f, row(gla_bd_fwd), wd_b, row(gla_bd_bwd), l, nb,
                             n_groups, n_ctx_groups)
        xs = _mix_call(p, o_f, o_b, x_args, row(gmlp_ln_g), row(gmlp_ln_b), w_s_b, b_s_t, mod,
                       row(gla_out_g), row(g_post_mix), w_out_b, l, first, nb, group0,
                       n_groups, n_ctx_groups)
        xs = _ffn_call(xs, mod, row(g_pre_ffn), row(g_post_ffn), w_ffn_in_b, w_ffn_out_b,
                       l, last, nb, group0, n_groups, n_ctx_groups)
        x_args = (xs,)
    return xs
```

```python
import functools

import jax
import jax.numpy as jnp
from jax import lax
from jax.experimental import pallas as pl
from jax.experimental.pallas import tpu as pltpu

F32 = jnp.float32
BF16 = jnp.bfloat16

D_MODEL = 2048
CHUNK = 128
A_WIDTH = 1024
A_HEADS = 8
A_HEAD_DIM = A_WIDTH // A_HEADS
B_WIDTH = 1024
B_HEADS = 4
B_DV = B_WIDTH // B_HEADS
B_DK = B_DV // 2
B_KEY_W = B_HEADS * B_DK
GATE_RANK = 16
GATE_TAU = 16.0
LOG2_Q_SCALE = -3.5
FFN_HIDDEN = 5632
EPS = 1e-6

COL_U = 0
COL_V = A_WIDTH
COL_Q = 2 * A_WIDTH
COL_K = COL_Q + B_KEY_W
COL_VV = COL_K + B_KEY_W
COL_G = COL_VV + B_WIDTH
COL_D = COL_G + B_WIDTH
P_COLS = COL_D + 2 * GATE_RANK
MXU_WIDTH = 256
P_PAD = -(-P_COLS // MXU_WIDTH) * MXU_WIDTH
D_BLOCK = 128

FFN_TS = 1024
FFN_NS = -(-FFN_HIDDEN // FFN_TS)
FFN_LAST = FFN_HIDDEN - (FFN_NS - 1) * FFN_TS
FFN_TN = 512
FFN_NO = D_MODEL // FFN_TN
MOD_TN = 1024
MOD_ROWS = 8

VMEM_LIMIT = 56 * 1024 * 1024


def _params(*sem):
    return pltpu.CompilerParams(dimension_semantics=sem, vmem_limit_bytes=VMEM_LIMIT)


def _silu(x):
    return x * jax.nn.sigmoid(x)


def _gelu_tanh(x):
    k = 0.7978845608028654
    return x * (0.5 * (1.0 + jnp.tanh(k * (x + 0.044715 * (x * x * x)))))


def _rms(x):
    return x * lax.rsqrt(jnp.mean(x * x, axis=-1, keepdims=True) + EPS)


def _chunk_rows(rb):
    return pl.ds(pl.multiple_of(rb * CHUNK, CHUNK), CHUNK)


def _for_chunks(n, body):
    def step(rb, carry):
        body(rb)
        return carry
    lax.fori_loop(0, n, step, 0)


def _mod_row(group, rb, nb, n_ctx_groups):
    g = group + rb // nb
    return jnp.where(g < n_ctx_groups, nb, rb % nb)


def _load_canonical(ref):
    return lambda rb: ref[_chunk_rows(rb), :]


def _load_sample_major(ref, nb):
    return lambda rb: ref[rb % nb, _chunk_rows(rb // nb), :]


def _mod_kernel(act_ref, w_ref, b_ref, o_ref):
    a = _silu(act_ref[...])
    o_ref[...] = jnp.dot(a.astype(BF16), w_ref[...].astype(BF16),
                         preferred_element_type=F32) + b_ref[...]


def _mod_call(act, w_mod, b_mod):
    n_layers, d, n = w_mod.shape
    return pl.pallas_call(
        _mod_kernel,
        grid=(n_layers, n // MOD_TN),
        in_specs=[
            pl.BlockSpec((MOD_ROWS, d), lambda l, j: (0, 0)),
            pl.BlockSpec((None, d, MOD_TN), lambda l, j: (l, 0, j)),
            pl.BlockSpec((None, 1, MOD_TN), lambda l, j: (l, 0, j)),
        ],
        out_specs=pl.BlockSpec((None, MOD_ROWS, MOD_TN), lambda l, j: (l, 0, j)),
        out_shape=jax.ShapeDtypeStruct((n_layers, MOD_ROWS, n), F32),
        compiler_params=_params("arbitrary", "arbitrary"),
        name="mod",
    )(act, w_mod, b_mod.reshape(n_layers, 1, n))


def _x_specs(first, nb, groups, n_ctx_tiles, d, tile_of):
    if not first:
        return [pl.BlockSpec((groups * nb * CHUNK, d), lambda *ix: (tile_of(*ix), 0))]
    blk = (nb, groups * CHUNK, d)
    return [
        pl.BlockSpec(blk, lambda *ix: (0, jnp.maximum(tile_of(*ix) - n_ctx_tiles, 0), 0)),
        pl.BlockSpec(blk, lambda *ix: (0, jnp.minimum(tile_of(*ix), n_ctx_tiles - 1), 0),
                     pipeline_mode=pl.Buffered(1)),
    ]


def _with_source(first, tile, n_ctx_tiles, nb, x_refs, fn):
    if not first:
        fn(_load_canonical(x_refs[0]))
        return

    @pl.when(tile >= n_ctx_tiles)
    def _():
        fn(_load_sample_major(x_refs[0], nb))

    @pl.when(tile < n_ctx_tiles)
    def _():
        fn(_load_sample_major(x_refs[1], nb))


def _inproj_kernel(*refs, first, nb, n_groups, n_ctx_groups):
    n_x = 2 if first else 1
    x_refs = refs[:n_x]
    mod_ref, g_ref, w_ref, o_ref, h0_ref, h1_ref = refs[n_x:]
    s = pl.program_id(0)
    is_ctx = jnp.minimum(s, n_groups - 1) < n_ctx_groups

    def load(rb):
        if not first:
            return x_refs[0][rb * CHUNK:(rb + 1) * CHUNK, :]
        return jnp.where(is_ctx, x_refs[1][rb], x_refs[0][rb])

    def prepare(h_ref):
        for rb in range(nb):
            m = jnp.where(is_ctx, nb, rb)
            gain = g_ref[...] * (1.0 + mod_ref[m, 1:2, :])
            h = _rms(load(rb)) * gain + mod_ref[m, 0:1, :]
            h_ref[rb * CHUNK:(rb + 1) * CHUNK, :] = h.astype(BF16)

    def project(h_new, h_old):
        prepare(h_new)
        o_ref[...] = jnp.dot(h_old[...], w_ref[...], preferred_element_type=F32).astype(BF16)

    @pl.when(s == 0)
    def _():
        prepare(h0_ref)

    @pl.when((s > 0) & (s % 2 == 1))
    def _():
        project(h1_ref, h0_ref)

    @pl.when((s > 0) & (s % 2 == 0))
    def _():
        project(h0_ref, h1_ref)


def _inproj_call(x_args, mod, g_pre, w_in, layer, first, nb, n_groups, n_ctx_groups):
    d = D_MODEL
    rows = nb * CHUNK
    prep = lambda s: jnp.minimum(s, n_groups - 1)
    post = lambda s: jnp.maximum(s - 1, 0)
    kern = functools.partial(_inproj_kernel, first=first, nb=nb, n_groups=n_groups,
                             n_ctx_groups=n_ctx_groups)
    return pl.pallas_call(
        kern,
        grid=(n_groups + 1,),
        in_specs=_x_specs(first, nb, 1, n_ctx_groups, d, prep) + [
            pl.BlockSpec((None, MOD_ROWS, 6, d), lambda s: (layer, 0, 0, 0)),
            pl.BlockSpec((None, 1, d), lambda s: (layer, 0, 0)),
            pl.BlockSpec((None, d, P_PAD), lambda s: (layer, 0, 0),
                         pipeline_mode=pl.Buffered(1)),
        ],
        out_specs=pl.BlockSpec((rows, P_PAD), lambda s: (post(s), 0)),
        out_shape=jax.ShapeDtypeStruct((n_groups * rows, P_PAD), BF16),
        scratch_shapes=[pltpu.VMEM((rows, d), BF16), pltpu.VMEM((rows, d), BF16)],
        compiler_params=_params("arbitrary"),
        name="inproj",
    )(*x_args, mod, g_pre, w_in)


LOG2_E = 1.4426950408889634


def _log2_sigmoid(z):
    return jnp.minimum(z, 0.0) * LOG2_E - jnp.log2(1.0 + jnp.exp2(jnp.abs(z) * -LOG2_E))


def _cumsum_rows(tri, la):
    hi = la.astype(BF16)
    lo = (la - hi.astype(F32)).astype(BF16)
    return (jnp.dot(tri, hi, preferred_element_type=F32)
            + jnp.dot(tri, lo, preferred_element_type=F32))


def _gla_direction(q_ref, k_ref, v_ref, d_ref, wd_ref, bd_ref, s_ref, o_ref, reverse, nb):
    row = lax.broadcasted_iota(jnp.int32, (CHUNK, CHUNK), 0)
    col = lax.broadcasted_iota(jnp.int32, (CHUNK, CHUNK), 1)
    causal = (col >= row) if reverse else (col <= row)
    eye = col == row
    tri = jnp.where(causal, 1.0, 0.0).astype(BF16)
    end = 0 if reverse else CHUNK - 1
    mid = CHUNK // 2

    z = jnp.dot(d_ref[...], wd_ref[...], preferred_element_type=F32) + bd_ref[...]
    la_all = _log2_sigmoid(z) * (1.0 / GATE_TAU)

    for b in range(nb):
        rows = slice(b * CHUNK, (b + 1) * CHUNK)
        cum = _cumsum_rows(tri, la_all[rows])
        c_end = cum[end:end + 1, :]
        c_mid = cum[mid:mid + 1, :]
        e_mid = jnp.exp2(c_mid)
        e_end_mid = jnp.exp2(c_end - c_mid)
        e_end = jnp.exp2(c_end)
        c_mid_q = c_mid - LOG2_Q_SCALE
        for h in range(B_HEADS):
            kc = slice(h * B_DK, (h + 1) * B_DK)
            vc = slice(h * B_DV, (h + 1) * B_DV)
            q = q_ref[rows, kc].astype(F32)
            k = k_ref[rows, kc].astype(F32)
            v = v_ref[rows, vc]
            q_t = q * jnp.exp2(cum[:, kc] - c_mid_q[:, kc])
            k_t = k * jnp.exp2(c_mid[:, kc] - cum[:, kc])
            q_in = (q_t * e_mid[:, kc]).astype(BF16)
            k_d = k_t * e_end_mid[:, kc]
            scores = lax.dot_general(q_t.astype(BF16), k_t.astype(BF16),
                                     (((1,), (1,)), ((), ())), preferred_element_type=F32)
            scores = jnp.where(causal, scores, 0.0).astype(BF16)
            s = s_ref[b, h]
            o = jnp.dot(scores, v, preferred_element_type=F32)
            o = o + jnp.dot(q_in, s.astype(BF16), preferred_element_type=F32)
            o_ref[rows, vc] = o
            decay = jnp.broadcast_to(e_end[:, kc], (B_DK, B_DK))
            decay_col = jnp.sum(jnp.where(eye, decay, 0.0), axis=-1, keepdims=True)
            s_ref[b, h] = decay_col * s + jnp.dot(k_d.T.astype(BF16), v,
                                                  preferred_element_type=F32)


def _gla_kernel(qf_ref, kf_ref, vf_ref, df_ref, qb_ref, kb_ref, vb_ref, db_ref,
                wdf_ref, bdf_ref, wdb_ref, bdb_ref, of_ref, ob_ref, s_ref, *, nb):
    @pl.when(pl.program_id(0) == 0)
    def _():
        s_ref[...] = jnp.zeros_like(s_ref)

    _gla_direction(qf_ref, kf_ref, vf_ref, df_ref, wdf_ref, bdf_ref, s_ref.at[0], of_ref,
                   False, nb)
    _gla_direction(qb_ref, kb_ref, vb_ref, db_ref, wdb_ref, bdb_ref, s_ref.at[1], ob_ref,
                   True, nb)


def _gla_call(p, wd_f, bd_f, wd_b, bd_b, layer, nb, n_groups, n_ctx_groups):
    rows = nb * CHUNK

    def fwd(s):
        return s

    def bwd(s):
        return jnp.where(s < n_ctx_groups, n_ctx_groups - 1 - s, n_groups + n_ctx_groups - 1 - s)

    def chunk_specs(order):
        return [
            pl.BlockSpec((rows, B_KEY_W), lambda s: (order(s), COL_Q // B_KEY_W)),
            pl.BlockSpec((rows, B_KEY_W), lambda s: (order(s), COL_K // B_KEY_W)),
            pl.BlockSpec((rows, B_WIDTH), lambda s: (order(s), COL_VV // B_WIDTH)),
            pl.BlockSpec((rows, D_BLOCK), lambda s: (order(s), COL_D // D_BLOCK)),
        ]

    def weight_specs():
        return [
            pl.BlockSpec((None, D_BLOCK, B_KEY_W), lambda s: (layer, 0, 0)),
            pl.BlockSpec((None, 1, B_KEY_W), lambda s: (layer, 0, 0)),
        ]

    out = jax.ShapeDtypeStruct((n_groups * rows, B_WIDTH), F32)
    return pl.pallas_call(
        functools.partial(_gla_kernel, nb=nb),
        grid=(n_groups,),
        in_specs=chunk_specs(fwd) + chunk_specs(bwd) + weight_specs() + weight_specs(),
        out_specs=[
            pl.BlockSpec((rows, B_WIDTH), lambda s: (fwd(s), 0)),
            pl.BlockSpec((rows, B_WIDTH), lambda s: (bwd(s), 0)),
        ],
        out_shape=[out, out],
        scratch_shapes=[pltpu.VMEM((2, nb, B_HEADS, B_DK, B_DV), F32)],
        compiler_params=_params("arbitrary"),
        name="gla",
    )(p, p, p, p, p, p, p, p, wd_f, bd_f, wd_b, bd_b)


def _mix_prepare(u_ref, v_ref, of_ref, ob_ref, g_ref, lng_ref, lnb_ref, ws_ref, bs_ref,
                 og_ref, cat_ref, nb):
    for rb in range(nb):
        rows = slice(rb * CHUNK, (rb + 1) * CHUNK)
        for h in range(A_HEADS):
            cols = slice(h * A_HEAD_DIM, (h + 1) * A_HEAD_DIM)
            v = _gelu_tanh(v_ref[rows, cols].astype(F32))
            mu = jnp.mean(v, axis=-1, keepdims=True)
            vc = v - mu
            var = jnp.mean(vc * vc, axis=-1, keepdims=True)
            vn = vc * lax.rsqrt(var + EPS) * lng_ref[:, cols] + lnb_ref[:, cols]
            mixed = jnp.dot(ws_ref[h], vn.astype(BF16), preferred_element_type=F32)
            mixed = mixed + bs_ref[:, h:h + 1]
            u = _gelu_tanh(u_ref[rows, cols].astype(F32))
            cat_ref[rows, cols] = (u * mixed).astype(BF16)
        for h in range(B_HEADS):
            cols = slice(h * B_DV, (h + 1) * B_DV)
            o = of_ref[rows, cols] + ob_ref[rows, cols]
            y = _rms(o) * og_ref[:, cols]
            gated = y * _silu(g_ref[rows, cols].astype(F32))
            cat_ref[rows, A_WIDTH + h * B_DV:A_WIDTH + (h + 1) * B_DV] = gated.astype(BF16)


def _mix_kernel(*refs, first, nb, group0, n_ctx_groups):
    n_x = 2 if first else 1
    prep_refs = refs[:5]
    x_refs = refs[5:5 + n_x]
    (lng_ref, lnb_ref, ws_ref, bs_ref, mod_ref, og_ref, gpost_ref, wo_ref,
     o_ref, cat0_ref, cat1_ref, y_ref) = refs[5 + n_x:]
    s = pl.program_id(0)

    def prepare(cat_ref):
        _mix_prepare(*prep_refs, lng_ref, lnb_ref, ws_ref, bs_ref, og_ref, cat_ref, nb)

    def project(cat_new, cat_old):
        prepare(cat_new)
        y_ref[...] = jnp.dot(cat_old[...], wo_ref[...], preferred_element_type=F32)

    @pl.when(s == 0)
    def _():
        prepare(cat0_ref)

    @pl.when((s > 0) & (s % 2 == 1))
    def _():
        project(cat1_ref, cat0_ref)

    @pl.when((s > 0) & (s % 2 == 0))
    def _():
        project(cat0_ref, cat1_ref)

    @pl.when(s > 0)
    def _():
        group = s - 1 + group0

        def epilogue(load):
            def residual(rb):
                m = _mod_row(group, rb, nb, n_ctx_groups)
                gain = gpost_ref[...] * mod_ref[m, 2:3, :]
                o_ref[_chunk_rows(rb), :] = load(rb) + _rms(y_ref[_chunk_rows(rb), :]) * gain
            _for_chunks(nb, residual)
        _with_source(first, group, n_ctx_groups, nb, x_refs, epilogue)


def _mix_call(p, o_f, o_b, x_args, ln_g, ln_b, w_s, b_s, mod, out_g, g_post, w_out, layer,
              first, nb, group0, n_groups, n_ctx_groups):
    d = D_MODEL
    rows = nb * CHUNK
    n_tiles = n_groups - group0
    prep = lambda s: group0 + jnp.minimum(s, n_tiles - 1)
    post = lambda s: group0 + jnp.maximum(s - 1, 0)
    kern = functools.partial(_mix_kernel, first=first, nb=nb, group0=group0,
                             n_ctx_groups=n_ctx_groups)
    return pl.pallas_call(
        kern,
        grid=(n_tiles + 1,),
        in_specs=[
            pl.BlockSpec((rows, A_WIDTH), lambda s: (prep(s), COL_U // A_WIDTH)),
            pl.BlockSpec((rows, A_WIDTH), lambda s: (prep(s), COL_V // A_WIDTH)),
            pl.BlockSpec((rows, B_WIDTH), lambda s: (prep(s), 0)),
            pl.BlockSpec((rows, B_WIDTH), lambda s: (prep(s), 0)),
            pl.BlockSpec((rows, B_WIDTH), lambda s: (prep(s), COL_G // B_WIDTH)),
        ] + _x_specs(first, nb, 1, n_ctx_groups, d, post) + [
            pl.BlockSpec((None, 1, A_WIDTH), lambda s: (layer, 0, 0)),
            pl.BlockSpec((None, 1, A_WIDTH), lambda s: (layer, 0, 0)),
            pl.BlockSpec((None, A_HEADS, CHUNK, CHUNK), lambda s: (layer, 0, 0, 0)),
            pl.BlockSpec((None, CHUNK, A_HEADS), lambda s: (layer, 0, 0)),
            pl.BlockSpec((None, MOD_ROWS, 6, d), lambda s: (layer, 0, 0, 0)),
            pl.BlockSpec((None, 1, B_WIDTH), lambda s: (layer, 0, 0)),
            pl.BlockSpec((None, 1, d), lambda s: (layer, 0, 0)),
            pl.BlockSpec((None, A_WIDTH + B_WIDTH, d), lambda s: (layer, 0, 0),
                         pipeline_mode=pl.Buffered(1)),
        ],
        out_specs=pl.BlockSpec((rows, d), lambda s: (post(s), 0)),
        out_shape=jax.ShapeDtypeStruct((n_groups * rows, d), F32),
        scratch_shapes=[pltpu.VMEM((rows, A_WIDTH + B_WIDTH), BF16),
                        pltpu.VMEM((rows, A_WIDTH + B_WIDTH), BF16),
                        pltpu.VMEM((rows, d), F32)],
        input_output_aliases={} if first else {5: 0},
        compiler_params=_params("arbitrary"),
        name="mix",
    )(p, p, o_f, o_b, p, *x_args, ln_g, ln_b, w_s, b_s, mod, out_g, g_post, w_out)


def _ffn_kernel(x_ref, mod_ref, gpre_ref, gpost_ref, wa_ref, wb_ref, wo_ref, o_ref,
                h_ref, act_ref, *, last, nb, group0, n_ctx_groups):
    j = pl.program_id(1)
    group = pl.program_id(0) + group0

    @pl.when(j == 0)
    def _():
        def body(rb):
            m = _mod_row(group, rb, nb, n_ctx_groups)
            gain = gpre_ref[...] * (1.0 + mod_ref[m, 4:5, :])
            h = _rms(x_ref[_chunk_rows(rb), :]) * gain + mod_ref[m, 3:4, :]
            h_ref[_chunk_rows(rb), :] = h.astype(BF16)
        _for_chunks(nb, body)

    def hidden(width):
        h = h_ref[...]
        a = jnp.dot(h, wa_ref[:, 0:width], preferred_element_type=F32)
        b = jnp.dot(h, wb_ref[:, 0:width], preferred_element_type=F32)
        act_ref[j, :, 0:width] = (_silu(a) * b).astype(BF16)

    @pl.when(j < FFN_NS - 1)
    def _():
        hidden(FFN_TS)

    @pl.when(j == FFN_NS - 1)
    def _():
        hidden(FFN_LAST)

    for n in range(FFN_NO):
        @pl.when(j == FFN_NS + n)
        def _():
            cols = slice(n * FFN_TN, (n + 1) * FFN_TN)
            acc = None
            for k in range(FFN_NS):
                width = FFN_TS if k < FFN_NS - 1 else FFN_LAST
                part = jnp.dot(act_ref[k, :, 0:width], wo_ref[k * FFN_TS:k * FFN_TS + width, :],
                               preferred_element_type=F32)
                acc = part if acc is None else acc + part
            if last:
                for b in range(nb):
                    o_ref[b, :, cols] = acc[b * CHUNK:(b + 1) * CHUNK, :]
            else:
                o_ref[:, cols] = acc

    @pl.when(j == FFN_NS + FFN_NO - 1)
    def _():
        def body(rb):
            rows = _chunk_rows(rb)
            m = _mod_row(group, rb, nb, n_ctx_groups)
            gain = gpost_ref[...] * mod_ref[m, 5:6, :]
            y = o_ref[rb] if last else o_ref[rows, :]
            res = x_ref[rows, :] + _rms(y) * gain
            if last:
                o_ref[rb] = res
            else:
                o_ref[rows, :] = res
        _for_chunks(nb, body)


def _ffn_call(xs, mod, g_pre, g_post, w_ab, w_ffn_out, layer, last, nb, group0,
              n_groups, n_ctx_groups):
    d = D_MODEL
    rows = nb * CHUNK
    last_s = FFN_NS - 1
    tile = lambda i: i + group0
    if last:
        out_spec = pl.BlockSpec((nb, CHUNK, d), lambda i, j: (0, i, 0))
        out_shape = jax.ShapeDtypeStruct((nb, (n_groups - group0) * CHUNK, d), F32)
    else:
        out_spec = pl.BlockSpec((rows, d), lambda i, j: (tile(i), 0))
        out_shape = jax.ShapeDtypeStruct((n_groups * rows, d), F32)
    kern = functools.partial(_ffn_kernel, last=last, nb=nb, group0=group0,
                             n_ctx_groups=n_ctx_groups)
    return pl.pallas_call(
        kern,
        grid=(n_groups - group0, FFN_NS + FFN_NO),
        in_specs=[
            pl.BlockSpec((rows, d), lambda i, j: (tile(i), 0)),
            pl.BlockSpec((None, MOD_ROWS, 6, d), lambda i, j: (layer, 0, 0, 0)),
            pl.BlockSpec((None, 1, d), lambda i, j: (layer, 0, 0)),
            pl.BlockSpec((None, 1, d), lambda i, j: (layer, 0, 0)),
            pl.BlockSpec((None, d, FFN_TS), lambda i, j: (layer, 0, jnp.minimum(j, last_s))),
            pl.BlockSpec((None, d, FFN_TS),
                         lambda i, j: (layer, 0, FFN_NS + jnp.minimum(j, last_s))),
            pl.BlockSpec((None, FFN_HIDDEN, FFN_TN),
                         lambda i, j: (layer, 0, jnp.maximum(j - FFN_NS, 0))),
        ],
        out_specs=out_spec,
        out_shape=out_shape,
        scratch_shapes=[pltpu.VMEM((rows, d), BF16),
                        pltpu.VMEM((FFN_NS, rows, FFN_TS), BF16)],
        input_output_aliases={} if last else {0: 0},
        compiler_params=_params("arbitrary", "arbitrary"),
        name="ffn",
    )(xs, mod, g_pre, g_post, w_ab, w_ab, w_ffn_out)


def kernel(x, c, ctx, c_ctx, w_mod, b_mod, g_pre_mix, g_post_mix, g_pre_ffn, g_post_ffn, w_in, gmlp_ln_g, gmlp_ln_b, gmlp_ws, gmlp_bs, gla_wd2_fwd, gla_bd_fwd, gla_wd2_bwd, gla_bd_bwd, gla_out_g, w_out, w_ffn_in, w_ffn_out):
    nb, seq, d = x.shape
    ctx_len = ctx.shape[1]
    n_layers = w_mod.shape[0]
    assert d == D_MODEL and nb + 1 <= MOD_ROWS
    assert seq % CHUNK == 0 and ctx_len % CHUNK == 0
    n_ctx_groups = ctx_len // CHUNK
    n_groups = n_ctx_groups + seq // CHUNK

    act = jnp.zeros((MOD_ROWS, d), F32).at[:nb].set(c).at[nb].set(c_ctx)
    mod = _mod_call(act, w_mod, b_mod).reshape(n_layers, MOD_ROWS, 6, d)

    row = lambda t: t.reshape(n_layers, 1, t.shape[-1])
    w_in_b = jnp.pad(w_in, ((0, 0), (0, 0), (0, P_PAD - P_COLS))).astype(BF16)
    w_out_b = w_out.astype(BF16)
    hid_pad = ((0, 0), (0, 0), (0, FFN_NS * FFN_TS - FFN_HIDDEN))
    w_ffn_in_b = jnp.concatenate(
        [jnp.pad(w_ffn_in[:, :, :FFN_HIDDEN].astype(BF16), hid_pad),
         jnp.pad(w_ffn_in[:, :, FFN_HIDDEN:].astype(BF16), hid_pad)], axis=-1)
    w_ffn_out_b = w_ffn_out.astype(BF16)
    w_s_b = gmlp_ws.astype(BF16)
    b_s_t = jnp.swapaxes(gmlp_bs, 1, 2)
    wd_f = jnp.pad(gla_wd2_fwd, ((0, 0), (0, D_BLOCK - GATE_RANK), (0, 0))).astype(BF16)
    wd_b = jnp.pad(gla_wd2_bwd, ((0, 0), (GATE_RANK, D_BLOCK - 2 * GATE_RANK), (0, 0))).astype(BF16)

    x_args = (x, ctx)
    for l in range(n_layers):
        first = l == 0
        last = l == n_layers - 1
        group0 = n_ctx_groups if last else 0
        p = _inproj_call(x_args, mod, row(g_pre_mix), w_in_b, l, first, nb, n_groups,
                         n_ctx_groups)
        o_f, o_b = _gla_call(p, wd_f, row(gla_bd_fwd), wd_b, row(gla_bd_bwd), l, nb,
                             n_groups, n_ctx_groups)
        xs = _mix_call(p, o_f, o_b, x_args, row(gmlp_ln_g), row(gmlp_ln_b), w_s_b, b_s_t, mod,
                       row(gla_out_g), row(g_post_mix), w_out_b, l, first, nb, group0,
                       n_groups, n_ctx_groups)
        xs = _ffn_call(xs, mod, row(g_pre_ffn), row(g_post_ffn), w_ffn_in_b, w_ffn_out_b,
                       l, last, nb, group0, n_groups, n_ctx_groups)
        x_args = (xs,)
    return xs
```

```python
import functools

import jax
import jax.numpy as jnp
from jax import lax
from jax.experimental import pallas as pl
from jax.experimental.pallas import tpu as pltpu

F32 = jnp.float32
BF16 = jnp.bfloat16

D_MODEL = 2048
CHUNK = 128
A_WIDTH = 1024
A_HEADS = 8
A_HEAD_DIM = A_WIDTH // A_HEADS
B_WIDTH = 1024
B_HEADS = 4
B_DV = B_WIDTH // B_HEADS
B_DK = B_DV // 2
B_KEY_W = B_HEADS * B_DK
GATE_RANK = 16
GATE_TAU = 16.0
LOG2_Q_SCALE = -3.5
FFN_HIDDEN = 5632
EPS = 1e-6

COL_U = 0
COL_V = A_WIDTH
COL_Q = 2 * A_WIDTH
COL_K = COL_Q + B_KEY_W
COL_VV = COL_K + B_KEY_W
COL_G = COL_VV + B_WIDTH
COL_D = COL_G + B_WIDTH
P_COLS = COL_D + 2 * GATE_RANK
MXU_WIDTH = 256
P_PAD = -(-P_COLS // MXU_WIDTH) * MXU_WIDTH
D_BLOCK = 128

FFN_TH = 512
FFN_TN = 512
FFN_NH = FFN_HIDDEN // FFN_TH
FFN_NO = D_MODEL // FFN_TN
MOD_TN = 2048
MOD_ROWS = 8

VMEM_LIMIT = 56 * 1024 * 1024


def _params(*sem):
    return pltpu.CompilerParams(dimension_semantics=sem, vmem_limit_bytes=VMEM_LIMIT)


def _silu(x):
    return x * jax.nn.sigmoid(x)


def _gelu_tanh(x):
    k = 0.7978845608028654
    return x * (0.5 + 0.5 * jnp.tanh(x * (k + (k * 0.044715) * (x * x))))


def _rms(x):
    return x * lax.rsqrt(jnp.mean(x * x, axis=-1, keepdims=True) + EPS)


def _chunk_rows(rb):
    return pl.ds(pl.multiple_of(rb * CHUNK, CHUNK), CHUNK)


def _for_chunks(n, body):
    def step(rb, carry):
        body(rb)
        return carry
    lax.fori_loop(0, n, step, 0)


def _mod_row(group, rb, nb, n_ctx_groups):
    g = group + rb // nb
    return jnp.where(g < n_ctx_groups, nb, rb % nb)


def _load_canonical(ref):
    return lambda rb: ref[_chunk_rows(rb), :]


def _load_sample_major(ref, nb):
    return lambda rb: ref[rb % nb, _chunk_rows(rb // nb), :]


def _mod_kernel(act_ref, w_ref, b_ref, o_ref):
    a = _silu(act_ref[...])
    o_ref[...] = jnp.dot(a.astype(BF16), w_ref[...].astype(BF16),
                         preferred_element_type=F32) + b_ref[...]


def _mod_call(act, w_mod, b_mod):
    n_layers, d, n = w_mod.shape
    return pl.pallas_call(
        _mod_kernel,
        grid=(n_layers, n // MOD_TN),
        in_specs=[
            pl.BlockSpec((MOD_ROWS, d), lambda l, j: (0, 0)),
            pl.BlockSpec((None, d, MOD_TN), lambda l, j: (l, 0, j)),
            pl.BlockSpec((None, 1, MOD_TN), lambda l, j: (l, 0, j)),
        ],
        out_specs=pl.BlockSpec((None, MOD_ROWS, MOD_TN), lambda l, j: (l, 0, j)),
        out_shape=jax.ShapeDtypeStruct((n_layers, MOD_ROWS, n), F32),
        compiler_params=_params("arbitrary", "arbitrary"),
        name="mod",
    )(act, w_mod, b_mod.reshape(n_layers, 1, n))


def _x_specs(first, nb, groups, n_ctx_tiles, d, tile_of):
    if not first:
        return [pl.BlockSpec((groups * nb * CHUNK, d), lambda *ix: (tile_of(*ix), 0))]
    blk = (nb, groups * CHUNK, d)
    return [
        pl.BlockSpec(blk, lambda *ix: (0, jnp.maximum(tile_of(*ix) - n_ctx_tiles, 0), 0)),
        pl.BlockSpec(blk, lambda *ix: (0, jnp.minimum(tile_of(*ix), n_ctx_tiles - 1), 0),
                     pipeline_mode=pl.Buffered(1)),
    ]


def _with_source(first, tile, n_ctx_tiles, nb, x_refs, fn):
    if not first:
        fn(_load_canonical(x_refs[0]))
        return

    @pl.when(tile >= n_ctx_tiles)
    def _():
        fn(_load_sample_major(x_refs[0], nb))

    @pl.when(tile < n_ctx_tiles)
    def _():
        fn(_load_sample_major(x_refs[1], nb))


def _inproj_kernel(*refs, first, nb, n_groups, n_ctx_groups):
    n_x = 2 if first else 1
    x_refs = refs[:n_x]
    mod_ref, g_ref, w_ref, o_ref, h0_ref, h1_ref = refs[n_x:]
    s = pl.program_id(0)
    is_ctx = jnp.minimum(s, n_groups - 1) < n_ctx_groups

    def load(rb):
        if not first:
            return x_refs[0][rb * CHUNK:(rb + 1) * CHUNK, :]
        return jnp.where(is_ctx, x_refs[1][rb], x_refs[0][rb])

    def prepare(h_ref):
        for rb in range(nb):
            m = jnp.where(is_ctx, nb, rb)
            gain = g_ref[...] * (1.0 + mod_ref[m, 1:2, :])
            h = _rms(load(rb)) * gain + mod_ref[m, 0:1, :]
            h_ref[rb * CHUNK:(rb + 1) * CHUNK, :] = h.astype(BF16)

    def project(h_new, h_old):
        prepare(h_new)
        o_ref[...] = jnp.dot(h_old[...], w_ref[...], preferred_element_type=F32).astype(BF16)

    @pl.when(s == 0)
    def _():
        prepare(h0_ref)

    @pl.when((s > 0) & (s % 2 == 1))
    def _():
        project(h1_ref, h0_ref)

    @pl.when((s > 0) & (s % 2 == 0))
    def _():
        project(h0_ref, h1_ref)


def _inproj_call(x_args, mod, g_pre, w_in, layer, first, nb, n_groups, n_ctx_groups):
    d = D_MODEL
    rows = nb * CHUNK
    prep = lambda s: jnp.minimum(s, n_groups - 1)
    post = lambda s: jnp.maximum(s - 1, 0)
    kern = functools.partial(_inproj_kernel, first=first, nb=nb, n_groups=n_groups,
                             n_ctx_groups=n_ctx_groups)
    return pl.pallas_call(
        kern,
        grid=(n_groups + 1,),
        in_specs=_x_specs(first, nb, 1, n_ctx_groups, d, prep) + [
            pl.BlockSpec((None, MOD_ROWS, 6, d), lambda s: (layer, 0, 0, 0)),
            pl.BlockSpec((None, 1, d), lambda s: (layer, 0, 0)),
            pl.BlockSpec((None, d, P_PAD), lambda s: (layer, 0, 0),
                         pipeline_mode=pl.Buffered(1)),
        ],
        out_specs=pl.BlockSpec((rows, P_PAD), lambda s: (post(s), 0)),
        out_shape=jax.ShapeDtypeStruct((n_groups * rows, P_PAD), BF16),
        scratch_shapes=[pltpu.VMEM((rows, d), BF16), pltpu.VMEM((rows, d), BF16)],
        compiler_params=_params("arbitrary"),
        name="inproj",
    )(*x_args, mod, g_pre, w_in)


LOG2_E = 1.4426950408889634


def _log2_sigmoid(z):
    return jnp.minimum(z, 0.0) * LOG2_E - jnp.log2(1.0 + jnp.exp2(jnp.abs(z) * -LOG2_E))


def _cumsum_rows(tri, la):
    hi = la.astype(BF16)
    lo = (la - hi.astype(F32)).astype(BF16)
    return (jnp.dot(tri, hi, preferred_element_type=F32)
            + jnp.dot(tri, lo, preferred_element_type=F32))


def _gla_direction(q_ref, k_ref, v_ref, d_ref, wd_ref, bd_ref, s_ref, o_ref, reverse, nb):
    row = lax.broadcasted_iota(jnp.int32, (CHUNK, CHUNK), 0)
    col = lax.broadcasted_iota(jnp.int32, (CHUNK, CHUNK), 1)
    causal = (col >= row) if reverse else (col <= row)
    eye = col == row
    tri = jnp.where(causal, 1.0, 0.0).astype(BF16)
    end = 0 if reverse else CHUNK - 1
    mid = CHUNK // 2

    z = jnp.dot(d_ref[...], wd_ref[...], preferred_element_type=F32) + bd_ref[...]
    la_all = _log2_sigmoid(z) * (1.0 / GATE_TAU)

    for b in range(nb):
        rows = slice(b * CHUNK, (b + 1) * CHUNK)
        cum = _cumsum_rows(tri, la_all[rows])
        c_end = cum[end:end + 1, :]
        c_mid = cum[mid:mid + 1, :]
        e_mid = jnp.exp2(c_mid)
        e_end_mid = jnp.exp2(c_end - c_mid)
        e_end = jnp.exp2(c_end)
        c_mid_q = c_mid - LOG2_Q_SCALE
        for h in range(B_HEADS):
            kc = slice(h * B_DK, (h + 1) * B_DK)
            vc = slice(h * B_DV, (h + 1) * B_DV)
            q = q_ref[rows, kc].astype(F32)
            k = k_ref[rows, kc].astype(F32)
            v = v_ref[rows, vc]
            q_t = q * jnp.exp2(cum[:, kc] - c_mid_q[:, kc])
            k_t = k * jnp.exp2(c_mid[:, kc] - cum[:, kc])
            q_in = (q_t * e_mid[:, kc]).astype(BF16)
            k_d = k_t * e_end_mid[:, kc]
            scores = lax.dot_general(q_t.astype(BF16), k_t.astype(BF16),
                                     (((1,), (1,)), ((), ())), preferred_element_type=F32)
            scores = jnp.where(causal, scores, 0.0).astype(BF16)
            s = s_ref[b, h]
            o = jnp.dot(scores, v, preferred_element_type=F32)
            o = o + jnp.dot(q_in, s.astype(BF16), preferred_element_type=F32)
            o_ref[rows, vc] = o
            decay = jnp.broadcast_to(e_end[:, kc], (B_DK, B_DK))
            decay_col = jnp.sum(jnp.where(eye, decay, 0.0), axis=-1, keepdims=True)
            s_ref[b, h] = decay_col * s + jnp.dot(k_d.T.astype(BF16), v,
                                                  preferred_element_type=F32)


def _gla_kernel(qf_ref, kf_ref, vf_ref, df_ref, qb_ref, kb_ref, vb_ref, db_ref,
                wdf_ref, bdf_ref, wdb_ref, bdb_ref, of_ref, ob_ref, s_ref, *, nb):
    @pl.when(pl.program_id(0) == 0)
    def _():
        s_ref[...] = jnp.zeros_like(s_ref)

    _gla_direction(qf_ref, kf_ref, vf_ref, df_ref, wdf_ref, bdf_ref, s_ref.at[0], of_ref,
                   False, nb)
    _gla_direction(qb_ref, kb_ref, vb_ref, db_ref, wdb_ref, bdb_ref, s_ref.at[1], ob_ref,
                   True, nb)


def _gla_call(p, wd_f, bd_f, wd_b, bd_b, layer, nb, n_groups, n_ctx_groups):
    rows = nb * CHUNK

    def fwd(s):
        return s

    def bwd(s):
        return jnp.where(s < n_ctx_groups, n_ctx_groups - 1 - s, n_groups + n_ctx_groups - 1 - s)

    def chunk_specs(order):
        return [
            pl.BlockSpec((rows, B_KEY_W), lambda s: (order(s), COL_Q // B_KEY_W)),
            pl.BlockSpec((rows, B_KEY_W), lambda s: (order(s), COL_K // B_KEY_W)),
            pl.BlockSpec((rows, B_WIDTH), lambda s: (order(s), COL_VV // B_WIDTH)),
            pl.BlockSpec((rows, D_BLOCK), lambda s: (order(s), COL_D // D_BLOCK)),
        ]

    def weight_specs():
        return [
            pl.BlockSpec((None, D_BLOCK, B_KEY_W), lambda s: (layer, 0, 0)),
            pl.BlockSpec((None, 1, B_KEY_W), lambda s: (layer, 0, 0)),
        ]

    out = jax.ShapeDtypeStruct((n_groups * rows, B_WIDTH), F32)
    return pl.pallas_call(
        functools.partial(_gla_kernel, nb=nb),
        grid=(n_groups,),
        in_specs=chunk_specs(fwd) + chunk_specs(bwd) + weight_specs() + weight_specs(),
        out_specs=[
            pl.BlockSpec((rows, B_WIDTH), lambda s: (fwd(s), 0)),
            pl.BlockSpec((rows, B_WIDTH), lambda s: (bwd(s), 0)),
        ],
        out_shape=[out, out],
        scratch_shapes=[pltpu.VMEM((2, nb, B_HEADS, B_DK, B_DV), F32)],
        compiler_params=_params("arbitrary"),
        name="gla",
    )(p, p, p, p, p, p, p, p, wd_f, bd_f, wd_b, bd_b)


def _mix_prepare(u_ref, v_ref, of_ref, ob_ref, g_ref, lng_ref, lnb_ref, ws_ref, bs_ref,
                 og_ref, cat_ref, nb):
    for rb in range(nb):
        rows = slice(rb * CHUNK, (rb + 1) * CHUNK)
        for h in range(A_HEADS):
            cols = slice(h * A_HEAD_DIM, (h + 1) * A_HEAD_DIM)
            v = _gelu_tanh(v_ref[rows, cols].astype(F32))
            mu = jnp.mean(v, axis=-1, keepdims=True)
            vc = v - mu
            var = jnp.mean(vc * vc, axis=-1, keepdims=True)
            vn = vc * lax.rsqrt(var + EPS) * lng_ref[:, cols] + lnb_ref[:, cols]
            mixed = jnp.dot(ws_ref[h], vn.astype(BF16), preferred_element_type=F32)
            mixed = mixed + bs_ref[:, h:h + 1]
            u = _gelu_tanh(u_ref[rows, cols].astype(F32))
            cat_ref[rows, cols] = (u * mixed).astype(BF16)
        for h in range(B_HEADS):
            cols = slice(h * B_DV, (h + 1) * B_DV)
            o = of_ref[rows, cols] + ob_ref[rows, cols]
            y = _rms(o) * og_ref[:, cols]
            gated = y * _silu(g_ref[rows, cols].astype(F32))
            cat_ref[rows, A_WIDTH + h * B_DV:A_WIDTH + (h + 1) * B_DV] = gated.astype(BF16)


def _mix_kernel(*refs, first, nb, group0, n_ctx_groups):
    n_x = 2 if first else 1
    prep_refs = refs[:5]
    x_refs = refs[5:5 + n_x]
    (lng_ref, lnb_ref, ws_ref, bs_ref, mod_ref, og_ref, gpost_ref, wo_ref,
     o_ref, cat0_ref, cat1_ref, y_ref) = refs[5 + n_x:]
    s = pl.program_id(0)

    def prepare(cat_ref):
        _mix_prepare(*prep_refs, lng_ref, lnb_ref, ws_ref, bs_ref, og_ref, cat_ref, nb)

    def project(cat_new, cat_old):
        prepare(cat_new)
        y_ref[...] = jnp.dot(cat_old[...], wo_ref[...], preferred_element_type=F32)

    @pl.when(s == 0)
    def _():
        prepare(cat0_ref)

    @pl.when((s > 0) & (s % 2 == 1))
    def _():
        project(cat1_ref, cat0_ref)

    @pl.when((s > 0) & (s % 2 == 0))
    def _():
        project(cat0_ref, cat1_ref)

    @pl.when(s > 0)
    def _():
        group = s - 1 + group0

        def epilogue(load):
            def residual(rb):
                m = _mod_row(group, rb, nb, n_ctx_groups)
                gain = gpost_ref[...] * mod_ref[m, 2:3, :]
                o_ref[_chunk_rows(rb), :] = load(rb) + _rms(y_ref[_chunk_rows(rb), :]) * gain
            _for_chunks(nb, residual)
        _with_source(first, group, n_ctx_groups, nb, x_refs, epilogue)


def _mix_call(p, o_f, o_b, x_args, ln_g, ln_b, w_s, b_s, mod, out_g, g_post, w_out, layer,
              first, nb, group0, n_groups, n_ctx_groups):
    d = D_MODEL
    rows = nb * CHUNK
    n_tiles = n_groups - group0
    prep = lambda s: group0 + jnp.minimum(s, n_tiles - 1)
    post = lambda s: group0 + jnp.maximum(s - 1, 0)
    kern = functools.partial(_mix_kernel, first=first, nb=nb, group0=group0,
                             n_ctx_groups=n_ctx_groups)
    return pl.pallas_call(
        kern,
        grid=(n_tiles + 1,),
        in_specs=[
            pl.BlockSpec((rows, A_WIDTH), lambda s: (prep(s), COL_U // A_WIDTH)),
            pl.BlockSpec((rows, A_WIDTH), lambda s: (prep(s), COL_V // A_WIDTH)),
            pl.BlockSpec((rows, B_WIDTH), lambda s: (prep(s), 0)),
            pl.BlockSpec((rows, B_WIDTH), lambda s: (prep(s), 0)),
            pl.BlockSpec((rows, B_WIDTH), lambda s: (prep(s), COL_G // B_WIDTH)),
        ] + _x_specs(first, nb, 1, n_ctx_groups, d, post) + [
            pl.BlockSpec((None, 1, A_WIDTH), lambda s: (layer, 0, 0)),
            pl.BlockSpec((None, 1, A_WIDTH), lambda s: (layer, 0, 0)),
            pl.BlockSpec((None, A_HEADS, CHUNK, CHUNK), lambda s: (layer, 0, 0, 0)),
            pl.BlockSpec((None, CHUNK, A_HEADS), lambda s: (layer, 0, 0)),
            pl.BlockSpec((None, MOD_ROWS, 6, d), lambda s: (layer, 0, 0, 0)),
            pl.BlockSpec((None, 1, B_WIDTH), lambda s: (layer, 0, 0)),
            pl.BlockSpec((None, 1, d), lambda s: (layer, 0, 0)),
            pl.BlockSpec((None, A_WIDTH + B_WIDTH, d), lambda s: (layer, 0, 0),
                         pipeline_mode=pl.Buffered(1)),
        ],
        out_specs=pl.BlockSpec((rows, d), lambda s: (post(s), 0)),
        out_shape=jax.ShapeDtypeStruct((n_groups * rows, d), F32),
        scratch_shapes=[pltpu.VMEM((rows, A_WIDTH + B_WIDTH), BF16),
                        pltpu.VMEM((rows, A_WIDTH + B_WIDTH), BF16),
                        pltpu.VMEM((rows, d), F32)],
        input_output_aliases={} if first else {5: 0},
        compiler_params=_params("arbitrary"),
        name="mix",
    )(p, p, o_f, o_b, p, *x_args, ln_g, ln_b, w_s, b_s, mod, out_g, g_post, w_out)


def _ffn_kernel(x_ref, xn_ref, mod_ref, gpre_ref, gpost_ref, wa_ref, wb_ref, wo_ref, o_ref,
                h_ref, act_ref, *, last, nb, group0, n_groups, n_ctx_groups):
    i = pl.program_id(0)
    j = pl.program_id(1)
    group = i + group0
    next_group = jnp.minimum(group + 1, n_groups - 1)

    def normalise(src_ref, g, rb):
        rows = slice(rb * CHUNK, (rb + 1) * CHUNK)
        m = jnp.where(g < n_ctx_groups, nb, rb)
        gain = gpre_ref[...] * (1.0 + mod_ref[m, 4:5, :])
        h = _rms(src_ref[rows, :]) * gain + mod_ref[m, 3:4, :]
        h_ref[rows, :] = h.astype(BF16)

    @pl.when((i == 0) & (j == 0))
    def _():
        for rb in range(nb):
            normalise(x_ref, group, rb)

    @pl.when(j < FFN_NH)
    def _():
        h = h_ref[...]
        a = jnp.dot(h, wa_ref[...], preferred_element_type=F32)
        b = jnp.dot(h, wb_ref[...], preferred_element_type=F32)
        act_ref[j] = (_silu(a) * b).astype(BF16)

    for n in range(FFN_NO):
        @pl.when(j == FFN_NH + n)
        def _():
            cols = slice(n * FFN_TN, (n + 1) * FFN_TN)
            acc = jnp.dot(act_ref[0], wo_ref[0:FFN_TH, :], preferred_element_type=F32)
            for k in range(1, FFN_NH):
                acc += jnp.dot(act_ref[k], wo_ref[k * FFN_TH:(k + 1) * FFN_TH, :],
                               preferred_element_type=F32)
            if last:
                for b in range(nb):
                    o_ref[b, :, cols] = acc[b * CHUNK:(b + 1) * CHUNK, :]
            else:
                o_ref[:, cols] = acc
            for rb in range(n, nb, FFN_NO):
                normalise(xn_ref, next_group, rb)

    @pl.when(j == FFN_NH + FFN_NO - 1)
    def _():
        def body(rb):
            rows = _chunk_rows(rb)
            m = _mod_row(group, rb, nb, n_ctx_groups)
            gain = gpost_ref[...] * mod_ref[m, 5:6, :]
            y = o_ref[rb] if last else o_ref[rows, :]
            res = x_ref[rows, :] + _rms(y) * gain
            if last:
                o_ref[rb] = res
            else:
                o_ref[rows, :] = res
        _for_chunks(nb, body)


def _ffn_call(xs, mod, g_pre, g_post, w_ffn_in, w_ffn_out, layer, last, nb, group0,
              n_groups, n_ctx_groups):
    d = D_MODEL
    rows = nb * CHUNK
    last_h = FFN_NH - 1
    tile = lambda i: i + group0
    if last:
        out_spec = pl.BlockSpec((nb, CHUNK, d), lambda i, j: (0, i, 0))
        out_shape = jax.ShapeDtypeStruct((nb, (n_groups - group0) * CHUNK, d), F32)
    else:
        out_spec = pl.BlockSpec((rows, d), lambda i, j: (tile(i), 0))
        out_shape = jax.ShapeDtypeStruct((n_groups * rows, d), F32)
    kern = functools.partial(_ffn_kernel, last=last, nb=nb, group0=group0, n_groups=n_groups,
                             n_ctx_groups=n_ctx_groups)
    return pl.pallas_call(
        kern,
        grid=(n_groups - group0, FFN_NH + FFN_NO),
        in_specs=[
            pl.BlockSpec((rows, d), lambda i, j: (tile(i), 0)),
            pl.BlockSpec((rows, d), lambda i, j: (jnp.minimum(tile(i) + 1, n_groups - 1), 0)),
            pl.BlockSpec((None, MOD_ROWS, 6, d), lambda i, j: (layer, 0, 0, 0)),
            pl.BlockSpec((None, 1, d), lambda i, j: (layer, 0, 0)),
            pl.BlockSpec((None, 1, d), lambda i, j: (layer, 0, 0)),
            pl.BlockSpec((None, d, FFN_TH), lambda i, j: (layer, 0, jnp.minimum(j, last_h))),
            pl.BlockSpec((None, d, FFN_TH),
                         lambda i, j: (layer, 0, FFN_NH + jnp.minimum(j, last_h))),
            pl.BlockSpec((None, FFN_HIDDEN, FFN_TN),
                         lambda i, j: (layer, 0, jnp.maximum(j - FFN_NH, 0))),
        ],
        out_specs=out_spec,
        out_shape=out_shape,
        scratch_shapes=[pltpu.VMEM((rows, d), BF16),
                        pltpu.VMEM((FFN_NH, rows, FFN_TH), BF16)],
        compiler_params=_params("arbitrary", "arbitrary"),
        name="ffn",
    )(xs, xs, mod, g_pre, g_post, w_ffn_in, w_ffn_in, w_ffn_out)


def kernel(x, c, ctx, c_ctx, w_mod, b_mod, g_pre_mix, g_post_mix, g_pre_ffn, g_post_ffn, w_in, gmlp_ln_g, gmlp_ln_b, gmlp_ws, gmlp_bs, gla_wd2_fwd, gla_bd_fwd, gla_wd2_bwd, gla_bd_bwd, gla_out_g, w_out, w_ffn_in, w_ffn_out):
    nb, seq, d = x.shape
    ctx_len = ctx.shape[1]
    n_layers = w_mod.shape[0]
    assert d == D_MODEL and nb + 1 <= MOD_ROWS
    assert seq % CHUNK == 0 and ctx_len % CHUNK == 0
    n_ctx_groups = ctx_len // CHUNK
    n_groups = n_ctx_groups + seq // CHUNK

    act = jnp.zeros((MOD_ROWS, d), F32).at[:nb].set(c).at[nb].set(c_ctx)
    mod = _mod_call(act, w_mod, b_mod).reshape(n_layers, MOD_ROWS, 6, d)

    row = lambda t: t.reshape(n_layers, 1, t.shape[-1])
    w_in_b = jnp.pad(w_in, ((0, 0), (0, 0), (0, P_PAD - P_COLS))).astype(BF16)
    w_out_b = w_out.astype(BF16)
    w_ffn_in_b = w_ffn_in.astype(BF16)
    w_ffn_out_b = w_ffn_out.astype(BF16)
    w_s_b = gmlp_ws.astype(BF16)
    b_s_t = jnp.swapaxes(gmlp_bs, 1, 2)
    wd_f = jnp.pad(gla_wd2_fwd, ((0, 0), (0, D_BLOCK - GATE_RANK), (0, 0))).astype(BF16)
    wd_b = jnp.pad(gla_wd2_bwd, ((0, 0), (GATE_RANK, D_BLOCK - 2 * GATE_RANK), (0, 0))).astype(BF16)

    x_args = (x, ctx)
    for l in range(n_layers):
        first = l == 0
        last = l == n_layers - 1
        group0 = n_ctx_groups if last else 0
        p = _inproj_call(x_args, mod, row(g_pre_mix), w_in_b, l, first, nb, n_groups,
                         n_ctx_groups)
        o_f, o_b = _gla_call(p, wd_f, row(gla_bd_fwd), wd_b, row(gla_bd_bwd), l, nb,
                             n_groups, n_ctx_groups)
        xs = _mix_call(p, o_f, o_b, x_args, row(gmlp_ln_g), row(gmlp_ln_b), w_s_b, b_s_t, mod,
                       row(gla_out_g), row(g_post_mix), w_out_b, l, first, nb, group0,
                       n_groups, n_ctx_groups)
        xs = _ffn_call(xs, mod, row(g_pre_ffn), row(g_post_ffn), w_ffn_in_b, w_ffn_out_b,
                       l, last, nb, group0, n_groups, n_ctx_groups)
        x_args = (xs,)
    return xs
```

```python
import functools

import jax
import jax.numpy as jnp
from jax import lax
from jax.experimental import pallas as pl
from jax.experimental.pallas import tpu as pltpu

F32 = jnp.float32
BF16 = jnp.bfloat16

D_MODEL = 2048
CHUNK = 128
A_WIDTH = 1024
A_HEADS = 8
A_HEAD_DIM = A_WIDTH // A_HEADS
B_WIDTH = 1024
B_HEADS = 4
B_DV = B_WIDTH // B_HEADS
B_DK = B_DV // 2
B_KEY_W = B_HEADS * B_DK
GATE_RANK = 16
GATE_TAU = 16.0
LOG2_Q_SCALE = -3.5
FFN_HIDDEN = 5632
EPS = 1e-6

COL_U = 0
COL_V = A_WIDTH
COL_Q = 2 * A_WIDTH
COL_K = COL_Q + B_KEY_W
COL_VV = COL_K + B_KEY_W
COL_G = COL_VV + B_WIDTH
COL_D = COL_G + B_WIDTH
P_COLS = COL_D + 2 * GATE_RANK
MXU_WIDTH = 256
P_PAD = -(-P_COLS // MXU_WIDTH) * MXU_WIDTH
D_BLOCK = 128

FFN_TH = 512
FFN_TN = 512
FFN_NH = FFN_HIDDEN // FFN_TH
FFN_NO = D_MODEL // FFN_TN
FFN_HALF = FFN_HIDDEN // 2
FFN_TK = 256
MOD_TN = 1024
MOD_ROWS = 8

VMEM_LIMIT = 56 * 1024 * 1024


def _params(*sem):
    return pltpu.CompilerParams(dimension_semantics=sem, vmem_limit_bytes=VMEM_LIMIT)


def _silu(x):
    return x * jax.nn.sigmoid(x)


def _gelu_tanh(x):
    k = 0.7978845608028654
    return x * (0.5 + 0.5 * jnp.tanh(x * (k + (k * 0.044715) * (x * x))))


def _rms(x):
    return x * lax.rsqrt(jnp.mean(x * x, axis=-1, keepdims=True) + EPS)


def _chunk_rows(rb):
    return pl.ds(pl.multiple_of(rb * CHUNK, CHUNK), CHUNK)


def _for_chunks(n, body):
    def step(rb, carry):
        body(rb)
        return carry
    lax.fori_loop(0, n, step, 0)


def _mod_row(group, rb, nb, n_ctx_groups):
    g = group + rb // nb
    return jnp.where(g < n_ctx_groups, nb, rb % nb)


def _load_canonical(ref):
    return lambda rb: ref[_chunk_rows(rb), :]


def _load_sample_major(ref, nb):
    return lambda rb: ref[rb % nb, _chunk_rows(rb // nb), :]


def _mod_kernel(act_ref, w_ref, b_ref, o_ref):
    a = _silu(act_ref[...])
    o_ref[...] = jnp.dot(a.astype(BF16), w_ref[...].astype(BF16),
                         preferred_element_type=F32) + b_ref[...]


def _mod_call(act, w_mod, b_mod):
    n_layers, d, n = w_mod.shape
    return pl.pallas_call(
        _mod_kernel,
        grid=(n_layers, n // MOD_TN),
        in_specs=[
            pl.BlockSpec((MOD_ROWS, d), lambda l, j: (0, 0)),
            pl.BlockSpec((None, d, MOD_TN), lambda l, j: (l, 0, j)),
            pl.BlockSpec((None, 1, MOD_TN), lambda l, j: (l, 0, j)),
        ],
        out_specs=pl.BlockSpec((None, MOD_ROWS, MOD_TN), lambda l, j: (l, 0, j)),
        out_shape=jax.ShapeDtypeStruct((n_layers, MOD_ROWS, n), F32),
        compiler_params=_params("arbitrary", "arbitrary"),
        name="mod",
    )(act, w_mod, b_mod.reshape(n_layers, 1, n))


def _x_specs(first, nb, groups, n_ctx_tiles, d, tile_of):
    if not first:
        return [pl.BlockSpec((groups * nb * CHUNK, d), lambda *ix: (tile_of(*ix), 0))]
    blk = (nb, groups * CHUNK, d)
    return [
        pl.BlockSpec(blk, lambda *ix: (0, jnp.maximum(tile_of(*ix) - n_ctx_tiles, 0), 0)),
        pl.BlockSpec(blk, lambda *ix: (0, jnp.minimum(tile_of(*ix), n_ctx_tiles - 1), 0),
                     pipeline_mode=pl.Buffered(1)),
    ]


def _with_source(first, tile, n_ctx_tiles, nb, x_refs, fn):
    if not first:
        fn(_load_canonical(x_refs[0]))
        return

    @pl.when(tile >= n_ctx_tiles)
    def _():
        fn(_load_sample_major(x_refs[0], nb))

    @pl.when(tile < n_ctx_tiles)
    def _():
        fn(_load_sample_major(x_refs[1], nb))


def _inproj_kernel(*refs, first, nb, n_groups, n_ctx_groups):
    n_x = 2 if first else 1
    x_refs = refs[:n_x]
    mod_ref, g_ref, w_ref, o_ref, h0_ref, h1_ref = refs[n_x:]
    s = pl.program_id(0)
    is_ctx = jnp.minimum(s, n_groups - 1) < n_ctx_groups

    def load(rb):
        if not first:
            return x_refs[0][rb * CHUNK:(rb + 1) * CHUNK, :]
        return jnp.where(is_ctx, x_refs[1][rb], x_refs[0][rb])

    def prepare(h_ref):
        for rb in range(nb):
            m = jnp.where(is_ctx, nb, rb)
            gain = g_ref[...] * (1.0 + mod_ref[m, 1:2, :])
            h = _rms(load(rb)) * gain + mod_ref[m, 0:1, :]
            h_ref[rb * CHUNK:(rb + 1) * CHUNK, :] = h.astype(BF16)

    def project(h_new, h_old):
        prepare(h_new)
        o_ref[...] = jnp.dot(h_old[...], w_ref[...], preferred_element_type=F32).astype(BF16)

    @pl.when(s == 0)
    def _():
        prepare(h0_ref)

    @pl.when((s > 0) & (s % 2 == 1))
    def _():
        project(h1_ref, h0_ref)

    @pl.when((s > 0) & (s % 2 == 0))
    def _():
        project(h0_ref, h1_ref)


def _inproj_call(x_args, mod, g_pre, w_in, layer, first, nb, n_groups, n_ctx_groups):
    d = D_MODEL
    rows = nb * CHUNK
    prep = lambda s: jnp.minimum(s, n_groups - 1)
    post = lambda s: jnp.maximum(s - 1, 0)
    kern = functools.partial(_inproj_kernel, first=first, nb=nb, n_groups=n_groups,
                             n_ctx_groups=n_ctx_groups)
    return pl.pallas_call(
        kern,
        grid=(n_groups + 1,),
        in_specs=_x_specs(first, nb, 1, n_ctx_groups, d, prep) + [
            pl.BlockSpec((None, MOD_ROWS, 6, d), lambda s: (layer, 0, 0, 0)),
            pl.BlockSpec((None, 1, d), lambda s: (layer, 0, 0)),
            pl.BlockSpec((None, d, P_PAD), lambda s: (layer, 0, 0),
                         pipeline_mode=pl.Buffered(1)),
        ],
        out_specs=pl.BlockSpec((rows, P_PAD), lambda s: (post(s), 0)),
        out_shape=jax.ShapeDtypeStruct((n_groups * rows, P_PAD), BF16),
        scratch_shapes=[pltpu.VMEM((rows, d), BF16), pltpu.VMEM((rows, d), BF16)],
        compiler_params=_params("arbitrary"),
        name="inproj",
    )(*x_args, mod, g_pre, w_in)


LOG2_E = 1.4426950408889634


def _log2_sigmoid(z):
    return jnp.minimum(z, 0.0) * LOG2_E - jnp.log2(1.0 + jnp.exp2(jnp.abs(z) * -LOG2_E))


def _cumsum_rows(tri, la):
    hi = la.astype(BF16)
    lo = (la - hi.astype(F32)).astype(BF16)
    return (jnp.dot(tri, hi, preferred_element_type=F32)
            + jnp.dot(tri, lo, preferred_element_type=F32))


def _gla_direction(q_ref, k_ref, v_ref, d_ref, wd_ref, bd_ref, s_ref, o_ref, reverse, nb):
    row = lax.broadcasted_iota(jnp.int32, (CHUNK, CHUNK), 0)
    col = lax.broadcasted_iota(jnp.int32, (CHUNK, CHUNK), 1)
    causal = (col >= row) if reverse else (col <= row)
    eye = col == row
    tri = jnp.where(causal, 1.0, 0.0).astype(BF16)
    end = 0 if reverse else CHUNK - 1
    mid = CHUNK // 2

    z = jnp.dot(d_ref[...], wd_ref[...], preferred_element_type=F32) + bd_ref[...]
    la_all = _log2_sigmoid(z) * (1.0 / GATE_TAU)

    for b in range(nb):
        rows = slice(b * CHUNK, (b + 1) * CHUNK)
        cum = _cumsum_rows(tri, la_all[rows])
        c_end = cum[end:end + 1, :]
        c_mid = cum[mid:mid + 1, :]
        e_mid = jnp.exp2(c_mid)
        e_end_mid = jnp.exp2(c_end - c_mid)
        e_end = jnp.exp2(c_end)
        c_mid_q = c_mid - LOG2_Q_SCALE
        for h in range(B_HEADS):
            kc = slice(h * B_DK, (h + 1) * B_DK)
            vc = slice(h * B_DV, (h + 1) * B_DV)
            q = q_ref[rows, kc].astype(F32)
            k = k_ref[rows, kc].astype(F32)
            v = v_ref[rows, vc]
            q_t = q * jnp.exp2(cum[:, kc] - c_mid_q[:, kc])
            k_t = k * jnp.exp2(c_mid[:, kc] - cum[:, kc])
            q_in = (q_t * e_mid[:, kc]).astype(BF16)
            k_d = k_t * e_end_mid[:, kc]
            scores = lax.dot_general(q_t.astype(BF16), k_t.astype(BF16),
                                     (((1,), (1,)), ((), ())), preferred_element_type=F32)
            scores = jnp.where(causal, scores, 0.0).astype(BF16)
            s = s_ref[b, h]
            o = jnp.dot(scores, v, preferred_element_type=F32)
            o = o + jnp.dot(q_in, s.astype(BF16), preferred_element_type=F32)
            o_ref[rows, vc] = o
            decay = jnp.broadcast_to(e_end[:, kc], (B_DK, B_DK))
            decay_col = jnp.sum(jnp.where(eye, decay, 0.0), axis=-1, keepdims=True)
            s_ref[b, h] = decay_col * s + jnp.dot(k_d.T.astype(BF16), v,
                                                  preferred_element_type=F32)


def _gla_kernel(qf_ref, kf_ref, vf_ref, df_ref, qb_ref, kb_ref, vb_ref, db_ref,
                wdf_ref, bdf_ref, wdb_ref, bdb_ref, of_ref, ob_ref, s_ref, *, nb):
    @pl.when(pl.program_id(0) == 0)
    def _():
        s_ref[...] = jnp.zeros_like(s_ref)

    _gla_direction(qf_ref, kf_ref, vf_ref, df_ref, wdf_ref, bdf_ref, s_ref.at[0], of_ref,
                   False, nb)
    _gla_direction(qb_ref, kb_ref, vb_ref, db_ref, wdb_ref, bdb_ref, s_ref.at[1], ob_ref,
                   True, nb)


def _gla_call(p, wd_f, bd_f, wd_b, bd_b, layer, nb, n_groups, n_ctx_groups):
    rows = nb * CHUNK

    def fwd(s):
        return s

    def bwd(s):
        return jnp.where(s < n_ctx_groups, n_ctx_groups - 1 - s, n_groups + n_ctx_groups - 1 - s)

    def chunk_specs(order):
        return [
            pl.BlockSpec((rows, B_KEY_W), lambda s: (order(s), COL_Q // B_KEY_W)),
            pl.BlockSpec((rows, B_KEY_W), lambda s: (order(s), COL_K // B_KEY_W)),
            pl.BlockSpec((rows, B_WIDTH), lambda s: (order(s), COL_VV // B_WIDTH)),
            pl.BlockSpec((rows, D_BLOCK), lambda s: (order(s), COL_D // D_BLOCK)),
        ]

    def weight_specs():
        return [
            pl.BlockSpec((None, D_BLOCK, B_KEY_W), lambda s: (layer, 0, 0)),
            pl.BlockSpec((None, 1, B_KEY_W), lambda s: (layer, 0, 0)),
        ]

    out = jax.ShapeDtypeStruct((n_groups * rows, B_WIDTH), F32)
    return pl.pallas_call(
        functools.partial(_gla_kernel, nb=nb),
        grid=(n_groups,),
        in_specs=chunk_specs(fwd) + chunk_specs(bwd) + weight_specs() + weight_specs(),
        out_specs=[
            pl.BlockSpec((rows, B_WIDTH), lambda s: (fwd(s), 0)),
            pl.BlockSpec((rows, B_WIDTH), lambda s: (bwd(s), 0)),
        ],
        out_shape=[out, out],
        scratch_shapes=[pltpu.VMEM((2, nb, B_HEADS, B_DK, B_DV), F32)],
        compiler_params=_params("arbitrary"),
        name="gla",
    )(p, p, p, p, p, p, p, p, wd_f, bd_f, wd_b, bd_b)


def _mix_prepare(u_ref, v_ref, of_ref, ob_ref, g_ref, lng_ref, lnb_ref, ws_ref, bs_ref,
                 og_ref, cat_ref, nb):
    for rb in range(nb):
        rows = slice(rb * CHUNK, (rb + 1) * CHUNK)
        for h in range(A_HEADS):
            cols = slice(h * A_HEAD_DIM, (h + 1) * A_HEAD_DIM)
            v = _gelu_tanh(v_ref[rows, cols].astype(F32))
            mu = jnp.mean(v, axis=-1, keepdims=True)
            vc = v - mu
            var = jnp.mean(vc * vc, axis=-1, keepdims=True)
            vn = vc * lax.rsqrt(var + EPS) * lng_ref[:, cols] + lnb_ref[:, cols]
            mixed = jnp.dot(ws_ref[h], vn.astype(BF16), preferred_element_type=F32)
            mixed = mixed + bs_ref[:, h:h + 1]
            u = _gelu_tanh(u_ref[rows, cols].astype(F32))
            cat_ref[rows, cols] = (u * mixed).astype(BF16)
        for h in range(B_HEADS):
            cols = slice(h * B_DV, (h + 1) * B_DV)
            o = of_ref[rows, cols] + ob_ref[rows, cols]
            y = _rms(o) * og_ref[:, cols]
            gated = y * _silu(g_ref[rows, cols].astype(F32))
            cat_ref[rows, A_WIDTH + h * B_DV:A_WIDTH + (h + 1) * B_DV] = gated.astype(BF16)


def _mix_kernel(*refs, first, nb, group0, n_ctx_groups):
    n_x = 2 if first else 1
    prep_refs = refs[:5]
    x_refs = refs[5:5 + n_x]
    (lng_ref, lnb_ref, ws_ref, bs_ref, mod_ref, og_ref, gpost_ref, wo_ref,
     o_ref, cat0_ref, cat1_ref, y_ref) = refs[5 + n_x:]
    s = pl.program_id(0)

    def prepare(cat_ref):
        _mix_prepare(*prep_refs, lng_ref, lnb_ref, ws_ref, bs_ref, og_ref, cat_ref, nb)

    def project(cat_new, cat_old):
        prepare(cat_new)
        y_ref[...] = jnp.dot(cat_old[...], wo_ref[...], preferred_element_type=F32)

    @pl.when(s == 0)
    def _():
        prepare(cat0_ref)

    @pl.when((s > 0) & (s % 2 == 1))
    def _():
        project(cat1_ref, cat0_ref)

    @pl.when((s > 0) & (s % 2 == 0))
    def _():
        project(cat0_ref, cat1_ref)

    @pl.when(s > 0)
    def _():
        group = s - 1 + group0

        def epilogue(load):
            def residual(rb):
                m = _mod_row(group, rb, nb, n_ctx_groups)
                gain = gpost_ref[...] * mod_ref[m, 2:3, :]
                o_ref[_chunk_rows(rb), :] = load(rb) + _rms(y_ref[_chunk_rows(rb), :]) * gain
            _for_chunks(nb, residual)
        _with_source(first, group, n_ctx_groups, nb, x_refs, epilogue)


def _mix_call(p, o_f, o_b, x_args, ln_g, ln_b, w_s, b_s, mod, out_g, g_post, w_out, layer,
              first, nb, group0, n_groups, n_ctx_groups):
    d = D_MODEL
    rows = nb * CHUNK
    n_tiles = n_groups - group0
    prep = lambda s: group0 + jnp.minimum(s, n_tiles - 1)
    post = lambda s: group0 + jnp.maximum(s - 1, 0)
    kern = functools.partial(_mix_kernel, first=first, nb=nb, group0=group0,
                             n_ctx_groups=n_ctx_groups)
    return pl.pallas_call(
        kern,
        grid=(n_tiles + 1,),
        in_specs=[
            pl.BlockSpec((rows, A_WIDTH), lambda s: (prep(s), COL_U // A_WIDTH)),
            pl.BlockSpec((rows, A_WIDTH), lambda s: (prep(s), COL_V // A_WIDTH)),
            pl.BlockSpec((rows, B_WIDTH), lambda s: (prep(s), 0)),
            pl.BlockSpec((rows, B_WIDTH), lambda s: (prep(s), 0)),
            pl.BlockSpec((rows, B_WIDTH), lambda s: (prep(s), COL_G // B_WIDTH)),
        ] + _x_specs(first, nb, 1, n_ctx_groups, d, post) + [
            pl.BlockSpec((None, 1, A_WIDTH), lambda s: (layer, 0, 0)),
            pl.BlockSpec((None, 1, A_WIDTH), lambda s: (layer, 0, 0)),
            pl.BlockSpec((None, A_HEADS, CHUNK, CHUNK), lambda s: (layer, 0, 0, 0)),
            pl.BlockSpec((None, CHUNK, A_HEADS), lambda s: (layer, 0, 0)),
            pl.BlockSpec((None, MOD_ROWS, 6, d), lambda s: (layer, 0, 0, 0)),
            pl.BlockSpec((None, 1, B_WIDTH), lambda s: (layer, 0, 0)),
            pl.BlockSpec((None, 1, d), lambda s: (layer, 0, 0)),
            pl.BlockSpec((None, A_WIDTH + B_WIDTH, d), lambda s: (layer, 0, 0),
                         pipeline_mode=pl.Buffered(1)),
        ],
        out_specs=pl.BlockSpec((rows, d), lambda s: (post(s), 0)),
        out_shape=jax.ShapeDtypeStruct((n_groups * rows, d), F32),
        scratch_shapes=[pltpu.VMEM((rows, A_WIDTH + B_WIDTH), BF16),
                        pltpu.VMEM((rows, A_WIDTH + B_WIDTH), BF16),
                        pltpu.VMEM((rows, d), F32)],
        input_output_aliases={} if first else {5: 0},
        compiler_params=_params("arbitrary"),
        name="mix",
    )(p, p, o_f, o_b, p, *x_args, ln_g, ln_b, w_s, b_s, mod, out_g, g_post, w_out)


def _ffn_kernel(x_ref, mod_ref, gpre_ref, gpost_ref, wa_ref, wb_ref, wo_lo_ref, wo_hi_ref, o_ref,
                h_ref, act_ref, y_ref, *, last, nb, group0, n_ctx_groups):
    j = pl.program_id(1)
    group = pl.program_id(0) + group0

    @pl.when(j == 0)
    def _():
        def body(rb):
            m = _mod_row(group, rb, nb, n_ctx_groups)
            gain = gpre_ref[...] * (1.0 + mod_ref[m, 4:5, :])
            h = _rms(x_ref[_chunk_rows(rb), :]) * gain + mod_ref[m, 3:4, :]
            h_ref[_chunk_rows(rb), :] = h.astype(BF16)
        _for_chunks(nb, body)

    @pl.when(j < FFN_NH)
    def _():
        h = h_ref[...]
        a = jnp.dot(h, wa_ref[...], preferred_element_type=F32)
        b = jnp.dot(h, wb_ref[...], preferred_element_type=F32)
        act_ref[j] = (_silu(a) * b).astype(BF16)

    @pl.when(j >= FFN_NH)
    def _():
        acc = None
        for c in range(FFN_HIDDEN // FFN_TK):
            k, off = divmod(c * FFN_TK, FFN_TH)
            w_ref, row = (wo_lo_ref, c * FFN_TK) if c * FFN_TK < FFN_HALF else \
                         (wo_hi_ref, c * FFN_TK - FFN_HALF)
            part = jnp.dot(act_ref[k, :, off:off + FFN_TK], w_ref[row:row + FFN_TK, :],
                           preferred_element_type=F32)
            acc = part if acc is None else acc + part
        y_ref[j - FFN_NH] = acc

    @pl.when(j == FFN_NH + FFN_NO - 1)
    def _():
        def body(rb):
            rows = _chunk_rows(rb)
            m = _mod_row(group, rb, nb, n_ctx_groups)
            gain = gpost_ref[...] * mod_ref[m, 5:6, :]
            ys = [y_ref[n, rows, :] for n in range(FFN_NO)]
            ss = sum(jnp.sum(y * y, axis=-1, keepdims=True) for y in ys)
            inv = lax.rsqrt(ss * (1.0 / D_MODEL) + EPS)
            for n in range(FFN_NO):
                cols = slice(n * FFN_TN, (n + 1) * FFN_TN)
                res = x_ref[rows, cols] + ys[n] * inv * gain[:, cols]
                if last:
                    o_ref[rb, :, cols] = res
                else:
                    o_ref[rows, cols] = res
        _for_chunks(nb, body)


def _ffn_call(xs, mod, g_pre, g_post, w_ffn_in, w_ffn_out, layer, last, nb, group0,
              n_groups, n_ctx_groups):
    d = D_MODEL
    rows = nb * CHUNK
    last_h = FFN_NH - 1
    tile = lambda i: i + group0
    if last:
        out_spec = pl.BlockSpec((nb, CHUNK, d), lambda i, j: (0, i, 0))
        out_shape = jax.ShapeDtypeStruct((nb, (n_groups - group0) * CHUNK, d), F32)
    else:
        out_spec = pl.BlockSpec((rows, d), lambda i, j: (tile(i), 0))
        out_shape = jax.ShapeDtypeStruct((n_groups * rows, d), F32)
    kern = functools.partial(_ffn_kernel, last=last, nb=nb, group0=group0,
                             n_ctx_groups=n_ctx_groups)
    return pl.pallas_call(
        kern,
        grid=(n_groups - group0, FFN_NH + FFN_NO),
        in_specs=[
            pl.BlockSpec((rows, d), lambda i, j: (tile(i), 0)),
            pl.BlockSpec((None, MOD_ROWS, 6, d), lambda i, j: (layer, 0, 0, 0)),
            pl.BlockSpec((None, 1, d), lambda i, j: (layer, 0, 0)),
            pl.BlockSpec((None, 1, d), lambda i, j: (layer, 0, 0)),
            pl.BlockSpec((None, d, FFN_TH), lambda i, j: (layer, 0, jnp.minimum(j, last_h))),
            pl.BlockSpec((None, d, FFN_TH),
                         lambda i, j: (layer, 0, FFN_NH + jnp.minimum(j, last_h))),
            pl.BlockSpec((None, FFN_HALF, FFN_TN),
                         lambda i, j: (layer, 0, jnp.maximum(j - FFN_NH, 0))),
            pl.BlockSpec((None, FFN_HALF, FFN_TN),
                         lambda i, j: (layer, 1, jnp.maximum(j - FFN_NH, 0))),
        ],
        out_specs=out_spec,
        out_shape=out_shape,
        scratch_shapes=[pltpu.VMEM((rows, d), BF16),
                        pltpu.VMEM((FFN_NH, rows, FFN_TH), BF16),
                        pltpu.VMEM((FFN_NO, rows, FFN_TN), F32)],
        input_output_aliases={} if last else {0: 0},
        compiler_params=_params("arbitrary", "arbitrary"),
        name="ffn",
    )(xs, mod, g_pre, g_post, w_ffn_in, w_ffn_in, w_ffn_out, w_ffn_out)


def kernel(x, c, ctx, c_ctx, w_mod, b_mod, g_pre_mix, g_post_mix, g_pre_ffn, g_post_ffn, w_in, gmlp_ln_g, gmlp_ln_b, gmlp_ws, gmlp_bs, gla_wd2_fwd, gla_bd_fwd, gla_wd2_bwd, gla_bd_bwd, gla_out_g, w_out, w_ffn_in, w_ffn_out):
    nb, seq, d = x.shape
    ctx_len = ctx.shape[1]
    n_layers = w_mod.shape[0]
    assert d == D_MODEL and nb + 1 <= MOD_ROWS
    assert seq % CHUNK == 0 and ctx_len % CHUNK == 0
    n_ctx_groups = ctx_len // CHUNK
    n_groups = n_ctx_groups + seq // CHUNK

    act = jnp.zeros((MOD_ROWS, d), F32).at[:nb].set(c).at[nb].set(c_ctx)
    mod = _mod_call(act, w_mod, b_mod).reshape(n_layers, MOD_ROWS, 6, d)

    row = lambda t: t.reshape(n_layers, 1, t.shape[-1])
    w_in_b = jnp.pad(w_in, ((0, 0), (0, 0), (0, P_PAD - P_COLS))).astype(BF16)
    w_out_b = w_out.astype(BF16)
    w_ffn_in_b = w_ffn_in.astype(BF16)
    w_ffn_out_b = w_ffn_out.astype(BF16)
    w_s_b = gmlp_ws.astype(BF16)
    b_s_t = jnp.swapaxes(gmlp_bs, 1, 2)
    wd_f = jnp.pad(gla_wd2_fwd, ((0, 0), (0, D_BLOCK - GATE_RANK), (0, 0))).astype(BF16)
    wd_b = jnp.pad(gla_wd2_bwd, ((0, 0), (GATE_RANK, D_BLOCK - 2 * GATE_RANK), (0, 0))).astype(BF16)

    x_args = (x, ctx)
    for l in range(n_layers):
        first = l == 0
        last = l == n_layers - 1
        group0 = n_ctx_groups if last else 0
        p = _inproj_call(x_args, mod, row(g_pre_mix), w_in_b, l, first, nb, n_groups,
                         n_ctx_groups)
        o_f, o_b = _gla_call(p, wd_f, row(gla_bd_fwd), wd_b, row(gla_bd_bwd), l, nb,
                             n_groups, n_ctx_groups)
        xs = _mix_call(p, o_f, o_b, x_args, row(gmlp_ln_g), row(gmlp_ln_b), w_s_b, b_s_t, mod,
                       row(gla_out_g), row(g_post_mix), w_out_b, l, first, nb, group0,
                       n_groups, n_ctx_groups)
        xs = _ffn_call(xs, mod, row(g_pre_ffn), row(g_post_ffn), w_ffn_in_b, w_ffn_out_b,
                       l, last, nb, group0, n_groups, n_ctx_groups)
        x_args = (xs,)
    return xs
```

```python
import functools

import jax
import jax.numpy as jnp
from jax import lax
from jax.experimental import pallas as pl
from jax.experimental.pallas import tpu as pltpu

F32 = jnp.float32
BF16 = jnp.bfloat16

D_MODEL = 2048
CHUNK = 128
A_WIDTH = 1024
A_HEADS = 8
A_HEAD_DIM = A_WIDTH // A_HEADS
B_WIDTH = 1024
B_HEADS = 4
B_DV = B_WIDTH // B_HEADS
B_DK = B_DV // 2
B_KEY_W = B_HEADS * B_DK
GATE_RANK = 16
GATE_TAU = 16.0
LOG2_Q_SCALE = -3.5
FFN_HIDDEN = 5632
EPS = 1e-6

COL_U = 0
COL_V = A_WIDTH
COL_Q = 2 * A_WIDTH
COL_K = COL_Q + B_KEY_W
COL_VV = COL_K + B_KEY_W
COL_G = COL_VV + B_WIDTH
COL_D = COL_G + B_WIDTH
P_COLS = COL_D + 2 * GATE_RANK
MXU_WIDTH = 256
P_PAD = -(-P_COLS // MXU_WIDTH) * MXU_WIDTH
D_BLOCK = 128

FFN_TH = 512
FFN_TN = 512
FFN_NH = FFN_HIDDEN // FFN_TH
FFN_NO = D_MODEL // FFN_TN
MOD_TN = 1024
MOD_ROWS = 8

VMEM_BYTES = 64 * 1024 * 1024
VMEM_LIMIT = VMEM_BYTES - 8 * 1024 * 1024
FFN_VMEM_LIMIT = VMEM_BYTES - 4 * 1024 * 1024


def _params(*sem, vmem_limit=VMEM_LIMIT):
    return pltpu.CompilerParams(dimension_semantics=sem, vmem_limit_bytes=vmem_limit)


def _silu(x):
    return x * jax.nn.sigmoid(x)


def _gelu_tanh(x):
    k = 0.7978845608028654
    return x * (0.5 + 0.5 * jnp.tanh(x * (k + (k * 0.044715) * (x * x))))


def _rms(x):
    return x * lax.rsqrt(jnp.mean(x * x, axis=-1, keepdims=True) + EPS)


def _chunk_rows(rb):
    return pl.ds(pl.multiple_of(rb * CHUNK, CHUNK), CHUNK)


def _for_chunks(n, body):
    def step(rb, carry):
        body(rb)
        return carry
    lax.fori_loop(0, n, step, 0)


def _mod_row(group, rb, nb, n_ctx_groups):
    g = group + rb // nb
    return jnp.where(g < n_ctx_groups, nb, rb % nb)


def _load_canonical(ref):
    return lambda rb: ref[_chunk_rows(rb), :]


def _load_sample_major(ref, nb):
    return lambda rb: ref[rb % nb, _chunk_rows(rb // nb), :]


def _mod_kernel(act_ref, w_ref, b_ref, o_ref):
    a = _silu(act_ref[...])
    o_ref[...] = jnp.dot(a.astype(BF16), w_ref[...].astype(BF16),
                         preferred_element_type=F32) + b_ref[...]


def _mod_call(act, w_mod, b_mod):
    n_layers, d, n = w_mod.shape
    return pl.pallas_call(
        _mod_kernel,
        grid=(n_layers, n // MOD_TN),
        in_specs=[
            pl.BlockSpec((MOD_ROWS, d), lambda l, j: (0, 0)),
            pl.BlockSpec((None, d, MOD_TN), lambda l, j: (l, 0, j)),
            pl.BlockSpec((None, 1, MOD_TN), lambda l, j: (l, 0, j)),
        ],
        out_specs=pl.BlockSpec((None, MOD_ROWS, MOD_TN), lambda l, j: (l, 0, j)),
        out_shape=jax.ShapeDtypeStruct((n_layers, MOD_ROWS, n), F32),
        compiler_params=_params("arbitrary", "arbitrary"),
        name="mod",
    )(act, w_mod, b_mod.reshape(n_layers, 1, n))


def _x_specs(first, nb, groups, n_ctx_tiles, d, tile_of):
    if not first:
        return [pl.BlockSpec((groups * nb * CHUNK, d), lambda *ix: (tile_of(*ix), 0))]
    blk = (nb, groups * CHUNK, d)
    return [
        pl.BlockSpec(blk, lambda *ix: (0, jnp.maximum(tile_of(*ix) - n_ctx_tiles, 0), 0)),
        pl.BlockSpec(blk, lambda *ix: (0, jnp.minimum(tile_of(*ix), n_ctx_tiles - 1), 0),
                     pipeline_mode=pl.Buffered(1)),
    ]


def _with_source(first, tile, n_ctx_tiles, nb, x_refs, fn):
    if not first:
        fn(_load_canonical(x_refs[0]))
        return

    @pl.when(tile >= n_ctx_tiles)
    def _():
        fn(_load_sample_major(x_refs[0], nb))

    @pl.when(tile < n_ctx_tiles)
    def _():
        fn(_load_sample_major(x_refs[1], nb))


def _inproj_kernel(*refs, first, nb, n_groups, n_ctx_groups):
    n_x = 2 if first else 1
    x_refs = refs[:n_x]
    mod_ref, g_ref, w_ref, o_ref, h0_ref, h1_ref = refs[n_x:]
    s = pl.program_id(0)
    is_ctx = jnp.minimum(s, n_groups - 1) < n_ctx_groups

    def load(rb):
        if not first:
            return x_refs[0][rb * CHUNK:(rb + 1) * CHUNK, :]
        return jnp.where(is_ctx, x_refs[1][rb], x_refs[0][rb])

    def prepare(h_ref):
        for rb in range(nb):
            m = jnp.where(is_ctx, nb, rb)
            gain = g_ref[...] * (1.0 + mod_ref[m, 1:2, :])
            h = _rms(load(rb)) * gain + mod_ref[m, 0:1, :]
            h_ref[rb * CHUNK:(rb + 1) * CHUNK, :] = h.astype(BF16)

    def project(h_new, h_old):
        prepare(h_new)
        o_ref[...] = jnp.dot(h_old[...], w_ref[...], preferred_element_type=F32).astype(BF16)

    @pl.when(s == 0)
    def _():
        prepare(h0_ref)

    @pl.when((s > 0) & (s % 2 == 1))
    def _():
        project(h1_ref, h0_ref)

    @pl.when((s > 0) & (s % 2 == 0))
    def _():
        project(h0_ref, h1_ref)


def _inproj_call(x_args, mod, g_pre, w_in, layer, first, nb, n_groups, n_ctx_groups):
    d = D_MODEL
    rows = nb * CHUNK
    prep = lambda s: jnp.minimum(s, n_groups - 1)
    post = lambda s: jnp.maximum(s - 1, 0)
    kern = functools.partial(_inproj_kernel, first=first, nb=nb, n_groups=n_groups,
                             n_ctx_groups=n_ctx_groups)
    return pl.pallas_call(
        kern,
        grid=(n_groups + 1,),
        in_specs=_x_specs(first, nb, 1, n_ctx_groups, d, prep) + [
            pl.BlockSpec((None, MOD_ROWS, 6, d), lambda s: (layer, 0, 0, 0)),
            pl.BlockSpec((None, 1, d), lambda s: (layer, 0, 0)),
            pl.BlockSpec((None, d, P_PAD), lambda s: (layer, 0, 0),
                         pipeline_mode=pl.Buffered(1)),
        ],
        out_specs=pl.BlockSpec((rows, P_PAD), lambda s: (post(s), 0)),
        out_shape=jax.ShapeDtypeStruct((n_groups * rows, P_PAD), BF16),
        scratch_shapes=[pltpu.VMEM((rows, d), BF16), pltpu.VMEM((rows, d), BF16)],
        compiler_params=_params("arbitrary"),
        name="inproj",
    )(*x_args, mod, g_pre, w_in)


LOG2_E = 1.4426950408889634


def _log2_sigmoid(z):
    return jnp.minimum(z, 0.0) * LOG2_E - jnp.log2(1.0 + jnp.exp2(jnp.abs(z) * -LOG2_E))


def _cumsum_rows(tri, la):
    hi = la.astype(BF16)
    lo = (la - hi.astype(F32)).astype(BF16)
    return jnp.dot(jnp.concatenate([tri, tri], axis=1), jnp.concatenate([hi, lo], axis=0),
                   preferred_element_type=F32)


def _gla_direction(q_ref, k_ref, v_ref, d_ref, wd_ref, bd_ref, s_ref, o_ref, reverse, nb):
    row = lax.broadcasted_iota(jnp.int32, (CHUNK, CHUNK), 0)
    col = lax.broadcasted_iota(jnp.int32, (CHUNK, CHUNK), 1)
    causal = (col >= row) if reverse else (col <= row)
    eye = col == row
    tri = jnp.where(causal, 1.0, 0.0).astype(BF16)
    end = 0 if reverse else CHUNK - 1
    mid = CHUNK // 2

    z = jnp.dot(d_ref[...], wd_ref[...], preferred_element_type=F32) + bd_ref[...]
    la_all = _log2_sigmoid(z) * (1.0 / GATE_TAU)

    for b in range(nb):
        rows = slice(b * CHUNK, (b + 1) * CHUNK)
        cum = _cumsum_rows(tri, la_all[rows])
        c_end = cum[end:end + 1, :]
        c_mid = cum[mid:mid + 1, :]
        e_mid = jnp.exp2(c_mid)
        e_end_mid = jnp.exp2(c_end - c_mid)
        e_end = jnp.exp2(c_end)
        c_mid_q = c_mid - LOG2_Q_SCALE
        for h in range(B_HEADS):
            kc = slice(h * B_DK, (h + 1) * B_DK)
            vc = slice(h * B_DV, (h + 1) * B_DV)
            q = q_ref[rows, kc].astype(F32)
            k = k_ref[rows, kc].astype(F32)
            v = v_ref[rows, vc]
            q_t = q * jnp.exp2(cum[:, kc] - c_mid_q[:, kc])
            k_t = k * jnp.exp2(c_mid[:, kc] - cum[:, kc])
            q_in = (q_t * e_mid[:, kc]).astype(BF16)
            k_d = k_t * e_end_mid[:, kc]
            scores = lax.dot_general(q_t.astype(BF16), k_t.astype(BF16),
                                     (((1,), (1,)), ((), ())), preferred_element_type=F32)
            scores = jnp.where(causal, scores, 0.0).astype(BF16)
            s = s_ref[b, h]
            o = jnp.dot(scores, v, preferred_element_type=F32)
            o = o + jnp.dot(q_in, s.astype(BF16), preferred_element_type=F32)
            o_ref[rows, vc] = o
            decay = jnp.broadcast_to(e_end[:, kc], (B_DK, B_DK))
            decay_col = jnp.sum(jnp.where(eye, decay, 0.0), axis=-1, keepdims=True)
            s_ref[b, h] = decay_col * s + jnp.dot(k_d.T.astype(BF16), v,
                                                  preferred_element_type=F32)


def _gla_kernel(qf_ref, kf_ref, vf_ref, df_ref, qb_ref, kb_ref, vb_ref, db_ref,
                wdf_ref, bdf_ref, wdb_ref, bdb_ref, of_ref, ob_ref, s_ref, *, nb):
    @pl.when(pl.program_id(0) == 0)
    def _():
        s_ref[...] = jnp.zeros_like(s_ref)

    _gla_direction(qf_ref, kf_ref, vf_ref, df_ref, wdf_ref, bdf_ref, s_ref.at[0], of_ref,
                   False, nb)
    _gla_direction(qb_ref, kb_ref, vb_ref, db_ref, wdb_ref, bdb_ref, s_ref.at[1], ob_ref,
                   True, nb)


def _gla_call(p, wd_f, bd_f, wd_b, bd_b, layer, nb, n_groups, n_ctx_groups):
    rows = nb * CHUNK

    def fwd(s):
        return s

    def bwd(s):
        return jnp.where(s < n_ctx_groups, n_ctx_groups - 1 - s, n_groups + n_ctx_groups - 1 - s)

    def chunk_specs(order):
        return [
            pl.BlockSpec((rows, B_KEY_W), lambda s: (order(s), COL_Q // B_KEY_W)),
            pl.BlockSpec((rows, B_KEY_W), lambda s: (order(s), COL_K // B_KEY_W)),
            pl.BlockSpec((rows, B_WIDTH), lambda s: (order(s), COL_VV // B_WIDTH)),
            pl.BlockSpec((rows, D_BLOCK), lambda s: (order(s), COL_D // D_BLOCK)),
        ]

    def weight_specs():
        return [
            pl.BlockSpec((None, D_BLOCK, B_KEY_W), lambda s: (layer, 0, 0)),
            pl.BlockSpec((None, 1, B_KEY_W), lambda s: (layer, 0, 0)),
        ]

    out = jax.ShapeDtypeStruct((n_groups * rows, B_WIDTH), F32)
    return pl.pallas_call(
        functools.partial(_gla_kernel, nb=nb),
        grid=(n_groups,),
        in_specs=chunk_specs(fwd) + chunk_specs(bwd) + weight_specs() + weight_specs(),
        out_specs=[
            pl.BlockSpec((rows, B_WIDTH), lambda s: (fwd(s), 0)),
            pl.BlockSpec((rows, B_WIDTH), lambda s: (bwd(s), 0)),
        ],
        out_shape=[out, out],
        scratch_shapes=[pltpu.VMEM((2, nb, B_HEADS, B_DK, B_DV), F32)],
        compiler_params=_params("arbitrary"),
        name="gla",
    )(p, p, p, p, p, p, p, p, wd_f, bd_f, wd_b, bd_b)


def _mix_prepare(u_ref, v_ref, of_ref, ob_ref, g_ref, lng_ref, lnb_ref, ws_ref, bs_ref,
                 og_ref, cat_ref, nb):
    for rb in range(nb):
        rows = slice(rb * CHUNK, (rb + 1) * CHUNK)
        for h in range(A_HEADS):
            cols = slice(h * A_HEAD_DIM, (h + 1) * A_HEAD_DIM)
            v = _gelu_tanh(v_ref[rows, cols].astype(F32))
            mu = jnp.mean(v, axis=-1, keepdims=True)
            vc = v - mu
            var = jnp.mean(vc * vc, axis=-1, keepdims=True)
            vn = vc * lax.rsqrt(var + EPS) * lng_ref[:, cols] + lnb_ref[:, cols]
            mixed = jnp.dot(ws_ref[h], vn.astype(BF16), preferred_element_type=F32)
            mixed = mixed + bs_ref[:, h:h + 1]
            u = _gelu_tanh(u_ref[rows, cols].astype(F32))
            cat_ref[rows, cols] = (u * mixed).astype(BF16)
        for h in range(B_HEADS):
            cols = slice(h * B_DV, (h + 1) * B_DV)
            o = of_ref[rows, cols] + ob_ref[rows, cols]
            y = _rms(o) * og_ref[:, cols]
            gated = y * _silu(g_ref[rows, cols].astype(F32))
            cat_ref[rows, A_WIDTH + h * B_DV:A_WIDTH + (h + 1) * B_DV] = gated.astype(BF16)


def _mix_kernel(*refs, first, nb, group0, n_ctx_groups):
    n_x = 2 if first else 1
    prep_refs = refs[:5]
    x_refs = refs[5:5 + n_x]
    (lng_ref, lnb_ref, ws_ref, bs_ref, mod_ref, og_ref, gpost_ref, wo_ref,
     o_ref, cat0_ref, cat1_ref, y_ref) = refs[5 + n_x:]
    s = pl.program_id(0)

    def prepare(cat_ref):
        _mix_prepare(*prep_refs, lng_ref, lnb_ref, ws_ref, bs_ref, og_ref, cat_ref, nb)

    def project(cat_new, cat_old):
        prepare(cat_new)
        y_ref[...] = jnp.dot(cat_old[...], wo_ref[...], preferred_element_type=F32)

    @pl.when(s == 0)
    def _():
        prepare(cat0_ref)

    @pl.when((s > 0) & (s % 2 == 1))
    def _():
        project(cat1_ref, cat0_ref)

    @pl.when((s > 0) & (s % 2 == 0))
    def _():
        project(cat0_ref, cat1_ref)

    @pl.when(s > 0)
    def _():
        group = s - 1 + group0

        def epilogue(load):
            def residual(rb):
                m = _mod_row(group, rb, nb, n_ctx_groups)
                gain = gpost_ref[...] * mod_ref[m, 2:3, :]
                o_ref[_chunk_rows(rb), :] = load(rb) + _rms(y_ref[_chunk_rows(rb), :]) * gain
            _for_chunks(nb, residual)
        _with_source(first, group, n_ctx_groups, nb, x_refs, epilogue)


def _mix_call(p, o_f, o_b, x_args, ln_g, ln_b, w_s, b_s, mod, out_g, g_post, w_out, layer,
              first, nb, group0, n_groups, n_ctx_groups):
    d = D_MODEL
    rows = nb * CHUNK
    n_tiles = n_groups - group0
    prep = lambda s: group0 + jnp.minimum(s, n_tiles - 1)
    post = lambda s: group0 + jnp.maximum(s - 1, 0)
    kern = functools.partial(_mix_kernel, first=first, nb=nb, group0=group0,
                             n_ctx_groups=n_ctx_groups)
    return pl.pallas_call(
        kern,
        grid=(n_tiles + 1,),
        in_specs=[
            pl.BlockSpec((rows, A_WIDTH), lambda s: (prep(s), COL_U // A_WIDTH)),
            pl.BlockSpec((rows, A_WIDTH), lambda s: (prep(s), COL_V // A_WIDTH)),
            pl.BlockSpec((rows, B_WIDTH), lambda s: (prep(s), 0)),
            pl.BlockSpec((rows, B_WIDTH), lambda s: (prep(s), 0)),
            pl.BlockSpec((rows, B_WIDTH), lambda s: (prep(s), COL_G // B_WIDTH)),
        ] + _x_specs(first, nb, 1, n_ctx_groups, d, post) + [
            pl.BlockSpec((None, 1, A_WIDTH), lambda s: (layer, 0, 0)),
            pl.BlockSpec((None, 1, A_WIDTH), lambda s: (layer, 0, 0)),
            pl.BlockSpec((None, A_HEADS, CHUNK, CHUNK), lambda s: (layer, 0, 0, 0)),
            pl.BlockSpec((None, CHUNK, A_HEADS), lambda s: (layer, 0, 0)),
            pl.BlockSpec((None, MOD_ROWS, 6, d), lambda s: (layer, 0, 0, 0)),
            pl.BlockSpec((None, 1, B_WIDTH), lambda s: (layer, 0, 0)),
            pl.BlockSpec((None, 1, d), lambda s: (layer, 0, 0)),
            pl.BlockSpec((None, A_WIDTH + B_WIDTH, d), lambda s: (layer, 0, 0),
                         pipeline_mode=pl.Buffered(1)),
        ],
        out_specs=pl.BlockSpec((rows, d), lambda s: (post(s), 0)),
        out_shape=jax.ShapeDtypeStruct((n_groups * rows, d), F32),
        scratch_shapes=[pltpu.VMEM((rows, A_WIDTH + B_WIDTH), BF16),
                        pltpu.VMEM((rows, A_WIDTH + B_WIDTH), BF16),
                        pltpu.VMEM((rows, d), F32)],
        input_output_aliases={} if first else {5: 0},
        compiler_params=_params("arbitrary"),
        name="mix",
    )(p, p, o_f, o_b, p, *x_args, ln_g, ln_b, w_s, b_s, mod, out_g, g_post, w_out)


def _ffn_kernel(x_ref, mod_ref, gpre_ref, gpost_ref, wa_ref, wb_ref, wo_ref, o_ref,
                h_ref, act_ref, *, last, nb, group0, n_ctx_groups):
    j = pl.program_id(1)
    group = pl.program_id(0) + group0

    @pl.when(j == 0)
    def _():
        def body(rb):
            m = _mod_row(group, rb, nb, n_ctx_groups)
            gain = gpre_ref[...] * (1.0 + mod_ref[m, 4:5, :])
            h = _rms(x_ref[_chunk_rows(rb), :]) * gain + mod_ref[m, 3:4, :]
            h_ref[_chunk_rows(rb), :] = h.astype(BF16)
        _for_chunks(nb, body)

    @pl.when(j < FFN_NH)
    def _():
        h = h_ref[...]
        a = jnp.dot(h, wa_ref[...], preferred_element_type=F32)
        b = jnp.dot(h, wb_ref[...], preferred_element_type=F32)
        act_ref[j] = (_silu(a) * b).astype(BF16)

    @pl.when(j == FFN_NH)
    def _():
        for n in range(FFN_NO):
            cols = slice(n * FFN_TN, (n + 1) * FFN_TN)
            acc = jnp.dot(act_ref[0], wo_ref[0:FFN_TH, cols], preferred_element_type=F32)
            for k in range(1, FFN_NH):
                acc += jnp.dot(act_ref[k], wo_ref[k * FFN_TH:(k + 1) * FFN_TH, cols],
                               preferred_element_type=F32)
            if last:
                for b in range(nb):
                    o_ref[b, :, cols] = acc[b * CHUNK:(b + 1) * CHUNK, :]
            else:
                o_ref[:, cols] = acc

        def body(rb):
            rows = _chunk_rows(rb)
            m = _mod_row(group, rb, nb, n_ctx_groups)
            gain = gpost_ref[...] * mod_ref[m, 5:6, :]
            y = o_ref[rb] if last else o_ref[rows, :]
            res = x_ref[rows, :] + _rms(y) * gain
            if last:
                o_ref[rb] = res
            else:
                o_ref[rows, :] = res
        _for_chunks(nb, body)


def _ffn_call(xs, mod, g_pre, g_post, w_ffn_in, w_ffn_out, layer, last, nb, group0,
              n_groups, n_ctx_groups):
    d = D_MODEL
    rows = nb * CHUNK
    last_h = FFN_NH - 1
    tile = lambda i: i + group0
    if last:
        out_spec = pl.BlockSpec((nb, CHUNK, d), lambda i, j: (0, i, 0))
        out_shape = jax.ShapeDtypeStruct((nb, (n_groups - group0) * CHUNK, d), F32)
    else:
        out_spec = pl.BlockSpec((rows, d), lambda i, j: (tile(i), 0))
        out_shape = jax.ShapeDtypeStruct((n_groups * rows, d), F32)
    kern = functools.partial(_ffn_kernel, last=last, nb=nb, group0=group0,
                             n_ctx_groups=n_ctx_groups)
    return pl.pallas_call(
        kern,
        grid=(n_groups - group0, FFN_NH + 1),
        in_specs=[
            pl.BlockSpec((rows, d), lambda i, j: (tile(i), 0)),
            pl.BlockSpec((None, MOD_ROWS, 6, d), lambda i, j: (layer, 0, 0, 0)),
            pl.BlockSpec((None, 1, d), lambda i, j: (layer, 0, 0)),
            pl.BlockSpec((None, 1, d), lambda i, j: (layer, 0, 0)),
            pl.BlockSpec((None, d, FFN_TH), lambda i, j: (layer, 0, jnp.minimum(j, last_h))),
            pl.BlockSpec((None, d, FFN_TH),
                         lambda i, j: (layer, 0, FFN_NH + jnp.minimum(j, last_h))),
            pl.BlockSpec((None, FFN_HIDDEN, d), lambda i, j: (layer, 0, 0),
                         pipeline_mode=pl.Buffered(1)),
        ],
        out_specs=out_spec,
        out_shape=out_shape,
        scratch_shapes=[pltpu.VMEM((rows, d), BF16),
                        pltpu.VMEM((FFN_NH, rows, FFN_TH), BF16)],
        input_output_aliases={} if last else {0: 0},
        compiler_params=_params("arbitrary", "arbitrary", vmem_limit=FFN_VMEM_LIMIT),
        name="ffn",
    )(xs, mod, g_pre, g_post, w_ffn_in, w_ffn_in, w_ffn_out)


def kernel(x, c, ctx, c_ctx, w_mod, b_mod, g_pre_mix, g_post_mix, g_pre_ffn, g_post_ffn, w_in, gmlp_ln_g, gmlp_ln_b, gmlp_ws, gmlp_bs, gla_wd2_fwd, gla_bd_fwd, gla_wd2_bwd, gla_bd_bwd, gla_out_g, w_out, w_ffn_in, w_ffn_out):
    nb, seq, d = x.shape
    ctx_len = ctx.shape[1]
    n_layers = w_mod.shape[0]
    assert d == D_MODEL and nb + 1 <= MOD_ROWS
    assert seq % CHUNK == 0 and ctx_len % CHUNK == 0
    n_ctx_groups = ctx_len // CHUNK
    n_groups = n_ctx_groups + seq // CHUNK

    act = jnp.zeros((MOD_ROWS, d), F32).at[:nb].set(c).at[nb].set(c_ctx)
    mod = _mod_call(act, w_mod, b_mod).reshape(n_layers, MOD_ROWS, 6, d)

    row = lambda t: t.reshape(n_layers, 1, t.shape[-1])
    w_in_b = jnp.pad(w_in, ((0, 0), (0, 0), (0, P_PAD - P_COLS))).astype(BF16)
    w_out_b = w_out.astype(BF16)
    w_ffn_in_b = w_ffn_in.astype(BF16)
    w_ffn_out_b = w_ffn_out.astype(BF16)
    w_s_b = gmlp_ws.astype(BF16)
    b_s_t = jnp.swapaxes(gmlp_bs, 1, 2)
    wd_f = jnp.pad(gla_wd2_fwd, ((0, 0), (0, D_BLOCK - GATE_RANK), (0, 0))).astype(BF16)
    wd_b = jnp.pad(gla_wd2_bwd, ((0, 0), (GATE_RANK, D_BLOCK - 2 * GATE_RANK), (0, 0))).astype(BF16)

    x_args = (x, ctx)
    for l in range(n_layers):
        first = l == 0
        last = l == n_layers - 1
        group0 = n_ctx_groups if last else 0
        p = _inproj_call(x_args, mod, row(g_pre_mix), w_in_b, l, first, nb, n_groups,
                         n_ctx_groups)
        o_f, o_b = _gla_call(p, wd_f, row(gla_bd_fwd), wd_b, row(gla_bd_bwd), l, nb,
                             n_groups, n_ctx_groups)
        xs = _mix_call(p, o_f, o_b, x_args, row(gmlp_ln_g), row(gmlp_ln_b), w_s_b, b_s_t, mod,
                       row(gla_out_g), row(g_post_mix), w_out_b, l, first, nb, group0,
                       n_groups, n_ctx_groups)
        xs = _ffn_call(xs, mod, row(g_pre_ffn), row(g_post_ffn), w_ffn_in_b, w_ffn_out_b,
                       l, last, nb, group0, n_groups, n_ctx_groups)
        x_args = (xs,)
    return xs
```

```python
import functools

import jax
import jax.numpy as jnp
from jax import lax
from jax.experimental import pallas as pl
from jax.experimental.pallas import tpu as pltpu

F32 = jnp.float32
BF16 = jnp.bfloat16

D_MODEL = 2048
CHUNK = 128
A_WIDTH = 1024
A_HEADS = 8
A_HEAD_DIM = A_WIDTH // A_HEADS
B_WIDTH = 1024
B_HEADS = 4
B_DV = B_WIDTH // B_HEADS
B_DK = B_DV // 2
B_KEY_W = B_HEADS * B_DK
GATE_RANK = 16
GATE_TAU = 16.0
LOG2_Q_SCALE = -3.5
FFN_HIDDEN = 5632
EPS = 1e-6

COL_U = 0
COL_V = A_WIDTH
COL_Q = 2 * A_WIDTH
COL_K = COL_Q + B_KEY_W
COL_VV = COL_K + B_KEY_W
COL_G = COL_VV + B_WIDTH
COL_D = COL_G + B_WIDTH
P_COLS = COL_D + 2 * GATE_RANK
MXU_WIDTH = 256
P_PAD = -(-P_COLS // MXU_WIDTH) * MXU_WIDTH
D_BLOCK = 128

FFN_TH = 512
FFN_TN = 512
FFN_NH = FFN_HIDDEN // FFN_TH
FFN_NO = D_MODEL // FFN_TN
MOD_TN = 1024
MOD_ROWS = 8
NORM_COLS = 512

VMEM_BYTES = 64 * 1024 * 1024
VMEM_LIMIT = VMEM_BYTES - 8 * 1024 * 1024
FFN_VMEM_LIMIT = VMEM_BYTES - 4 * 1024 * 1024


def _params(*sem, vmem_limit=VMEM_LIMIT):
    return pltpu.CompilerParams(dimension_semantics=sem, vmem_limit_bytes=vmem_limit)


def _silu(x):
    return x * jax.nn.sigmoid(x)


def _gelu_tanh(x):
    k = 0.7978845608028654
    return x * (0.5 + 0.5 * jnp.tanh(x * (k + (k * 0.044715) * (x * x))))


def _rms(x):
    return x * lax.rsqrt(jnp.mean(x * x, axis=-1, keepdims=True) + EPS)


def _chunk_rows(rb):
    return pl.ds(pl.multiple_of(rb * CHUNK, CHUNK), CHUNK)


def _for_chunks(n, body):
    def step(rb, carry):
        body(rb)
        return carry
    lax.fori_loop(0, n, step, 0)


def _mod_row(group, rb, nb, n_ctx_groups):
    g = group + rb // nb
    return jnp.where(g < n_ctx_groups, nb, rb % nb)


def _load_canonical(ref):
    return lambda rb, cols=slice(None): ref[_chunk_rows(rb), cols]


def _load_sample_major(ref, nb):
    return lambda rb, cols=slice(None): ref[rb % nb, _chunk_rows(rb // nb), cols]


def _norm_residual(read_y, read_x, write, gain):
    blocks = [slice(c, c + NORM_COLS) for c in range(0, D_MODEL, NORM_COLS)]
    ss = None
    for cols in blocks:
        y = read_y(cols)
        part = jnp.sum(y * y, axis=-1, keepdims=True)
        ss = part if ss is None else ss + part
    inv = lax.rsqrt(ss * (1.0 / D_MODEL) + EPS)
    for cols in blocks:
        write(cols, read_x(cols) + read_y(cols) * inv * gain[:, cols])


def _mod_kernel(act_ref, w_ref, b_ref, o_ref):
    a = _silu(act_ref[...])
    o_ref[...] = jnp.dot(a.astype(BF16), w_ref[...].astype(BF16),
                         preferred_element_type=F32) + b_ref[...]


def _mod_call(act, w_mod, b_mod):
    n_layers, d, n = w_mod.shape
    return pl.pallas_call(
        _mod_kernel,
        grid=(n_layers, n // MOD_TN),
        in_specs=[
            pl.BlockSpec((MOD_ROWS, d), lambda l, j: (0, 0)),
            pl.BlockSpec((None, d, MOD_TN), lambda l, j: (l, 0, j)),
            pl.BlockSpec((None, 1, MOD_TN), lambda l, j: (l, 0, j)),
        ],
        out_specs=pl.BlockSpec((None, MOD_ROWS, MOD_TN), lambda l, j: (l, 0, j)),
        out_shape=jax.ShapeDtypeStruct((n_layers, MOD_ROWS, n), F32),
        compiler_params=_params("arbitrary", "arbitrary"),
        name="mod",
    )(act, w_mod, b_mod.reshape(n_layers, 1, n))


def _x_specs(first, nb, groups, n_ctx_tiles, d, tile_of):
    if not first:
        return [pl.BlockSpec((groups * nb * CHUNK, d), lambda *ix: (tile_of(*ix), 0))]
    blk = (nb, groups * CHUNK, d)
    return [
        pl.BlockSpec(blk, lambda *ix: (0, jnp.maximum(tile_of(*ix) - n_ctx_tiles, 0), 0)),
        pl.BlockSpec(blk, lambda *ix: (0, jnp.minimum(tile_of(*ix), n_ctx_tiles - 1), 0),
                     pipeline_mode=pl.Buffered(1)),
    ]


def _with_source(first, tile, n_ctx_tiles, nb, x_refs, fn):
    if not first:
        fn(_load_canonical(x_refs[0]))
        return

    @pl.when(tile >= n_ctx_tiles)
    def _():
        fn(_load_sample_major(x_refs[0], nb))

    @pl.when(tile < n_ctx_tiles)
    def _():
        fn(_load_sample_major(x_refs[1], nb))


def _inproj_kernel(*refs, first, nb, n_groups, n_ctx_groups):
    n_x = 2 if first else 1
    x_refs = refs[:n_x]
    mod_ref, g_ref, w_ref, o_ref, h0_ref, h1_ref = refs[n_x:]
    s = pl.program_id(0)
    is_ctx = jnp.minimum(s, n_groups - 1) < n_ctx_groups

    def load(rb):
        if not first:
            return x_refs[0][rb * CHUNK:(rb + 1) * CHUNK, :]
        return jnp.where(is_ctx, x_refs[1][rb], x_refs[0][rb])

    def prepare(h_ref):
        for rb in range(nb):
            m = jnp.where(is_ctx, nb, rb)
            gain = g_ref[...] * (1.0 + mod_ref[m, 1:2, :])
            h = _rms(load(rb)) * gain + mod_ref[m, 0:1, :]
            h_ref[rb * CHUNK:(rb + 1) * CHUNK, :] = h.astype(BF16)

    def project(h_new, h_old):
        prepare(h_new)
        o_ref[...] = jnp.dot(h_old[...], w_ref[...], preferred_element_type=F32).astype(BF16)

    @pl.when(s == 0)
    def _():
        prepare(h0_ref)

    @pl.when((s > 0) & (s % 2 == 1))
    def _():
        project(h1_ref, h0_ref)

    @pl.when((s > 0) & (s % 2 == 0))
    def _():
        project(h0_ref, h1_ref)


def _inproj_call(x_args, mod, g_pre, w_in, layer, first, nb, n_groups, n_ctx_groups):
    d = D_MODEL
    rows = nb * CHUNK
    prep = lambda s: jnp.minimum(s, n_groups - 1)
    post = lambda s: jnp.maximum(s - 1, 0)
    kern = functools.partial(_inproj_kernel, first=first, nb=nb, n_groups=n_groups,
                             n_ctx_groups=n_ctx_groups)
    return pl.pallas_call(
        kern,
        grid=(n_groups + 1,),
        in_specs=_x_specs(first, nb, 1, n_ctx_groups, d, prep) + [
            pl.BlockSpec((None, MOD_ROWS, 6, d), lambda s: (layer, 0, 0, 0)),
            pl.BlockSpec((None, 1, d), lambda s: (layer, 0, 0)),
            pl.BlockSpec((None, d, P_PAD), lambda s: (layer, 0, 0),
                         pipeline_mode=pl.Buffered(1)),
        ],
        out_specs=pl.BlockSpec((rows, P_PAD), lambda s: (post(s), 0)),
        out_shape=jax.ShapeDtypeStruct((n_groups * rows, P_PAD), BF16),
        scratch_shapes=[pltpu.VMEM((rows, d), BF16), pltpu.VMEM((rows, d), BF16)],
        compiler_params=_params("arbitrary"),
        name="inproj",
    )(*x_args, mod, g_pre, w_in)


LOG2_E = 1.4426950408889634


def _log2_sigmoid(z):
    return jnp.minimum(z, 0.0) * LOG2_E - jnp.log2(1.0 + jnp.exp2(jnp.abs(z) * -LOG2_E))


def _cumsum_rows(tri, la):
    hi = la.astype(BF16)
    lo = (la - hi.astype(F32)).astype(BF16)
    return jnp.dot(jnp.concatenate([tri, tri], axis=1), jnp.concatenate([hi, lo], axis=0),
                   preferred_element_type=F32)


def _gla_direction(q_ref, k_ref, v_ref, d_ref, wd_ref, bd_ref, s_ref, o_ref, reverse, nb):
    row = lax.broadcasted_iota(jnp.int32, (CHUNK, CHUNK), 0)
    col = lax.broadcasted_iota(jnp.int32, (CHUNK, CHUNK), 1)
    causal = (col >= row) if reverse else (col <= row)
    eye = col == row
    tri = jnp.where(causal, 1.0, 0.0).astype(BF16)
    end = 0 if reverse else CHUNK - 1
    mid = CHUNK // 2

    z = jnp.dot(d_ref[...], wd_ref[...], preferred_element_type=F32) + bd_ref[...]
    la_all = _log2_sigmoid(z) * (1.0 / GATE_TAU)

    for b in range(nb):
        rows = slice(b * CHUNK, (b + 1) * CHUNK)
        cum = _cumsum_rows(tri, la_all[rows])
        c_end = cum[end:end + 1, :]
        c_mid = cum[mid:mid + 1, :]
        e_mid = jnp.exp2(c_mid)
        e_end_mid = jnp.exp2(c_end - c_mid)
        e_end = jnp.exp2(c_end)
        c_mid_q = c_mid - LOG2_Q_SCALE
        for h in range(B_HEADS):
            kc = slice(h * B_DK, (h + 1) * B_DK)
            vc = slice(h * B_DV, (h + 1) * B_DV)
            q = q_ref[rows, kc].astype(F32)
            k = k_ref[rows, kc].astype(F32)
            v = v_ref[rows, vc]
            q_t = q * jnp.exp2(cum[:, kc] - c_mid_q[:, kc])
            k_t = k * jnp.exp2(c_mid[:, kc] - cum[:, kc])
            q_in = (q_t * e_mid[:, kc]).astype(BF16)
            k_d = k_t * e_end_mid[:, kc]
            scores = lax.dot_general(q_t.astype(BF16), k_t.astype(BF16),
                                     (((1,), (1,)), ((), ())), preferred_element_type=F32)
            scores = jnp.where(causal, scores, 0.0).astype(BF16)
            s = s_ref[b, h]
            o = jnp.dot(scores, v, preferred_element_type=F32)
            o = o + jnp.dot(q_in, s.astype(BF16), preferred_element_type=F32)
            o_ref[rows, vc] = o
            decay = jnp.broadcast_to(e_end[:, kc], (B_DK, B_DK))
            decay_col = jnp.sum(jnp.where(eye, decay, 0.0), axis=-1, keepdims=True)
            s_ref[b, h] = decay_col * s + jnp.dot(k_d.T.astype(BF16), v,
                                                  preferred_element_type=F32)


def _gla_kernel(qf_ref, kf_ref, vf_ref, df_ref, qb_ref, kb_ref, vb_ref, db_ref,
                wdf_ref, bdf_ref, wdb_ref, bdb_ref, of_ref, ob_ref, s_ref, *, nb):
    @pl.when(pl.program_id(0) == 0)
    def _():
        s_ref[...] = jnp.zeros_like(s_ref)

    _gla_direction(qf_ref, kf_ref, vf_ref, df_ref, wdf_ref, bdf_ref, s_ref.at[0], of_ref,
                   False, nb)
    _gla_direction(qb_ref, kb_ref, vb_ref, db_ref, wdb_ref, bdb_ref, s_ref.at[1], ob_ref,
                   True, nb)


def _gla_call(p, wd_f, bd_f, wd_b, bd_b, layer, nb, n_groups, n_ctx_groups):
    rows = nb * CHUNK

    def fwd(s):
        return s

    def bwd(s):
        return jnp.where(s < n_ctx_groups, n_ctx_groups - 1 - s, n_groups + n_ctx_groups - 1 - s)

    def chunk_specs(order):
        return [
            pl.BlockSpec((rows, B_KEY_W), lambda s: (order(s), COL_Q // B_KEY_W)),
            pl.BlockSpec((rows, B_KEY_W), lambda s: (order(s), COL_K // B_KEY_W)),
            pl.BlockSpec((rows, B_WIDTH), lambda s: (order(s), COL_VV // B_WIDTH)),
            pl.BlockSpec((rows, D_BLOCK), lambda s: (order(s), COL_D // D_BLOCK)),
        ]

    def weight_specs():
        return [
            pl.BlockSpec((None, D_BLOCK, B_KEY_W), lambda s: (layer, 0, 0)),
            pl.BlockSpec((None, 1, B_KEY_W), lambda s: (layer, 0, 0)),
        ]

    out = jax.ShapeDtypeStruct((n_groups * rows, B_WIDTH), F32)
    return pl.pallas_call(
        functools.partial(_gla_kernel, nb=nb),
        grid=(n_groups,),
        in_specs=chunk_specs(fwd) + chunk_specs(bwd) + weight_specs() + weight_specs(),
        out_specs=[
            pl.BlockSpec((rows, B_WIDTH), lambda s: (fwd(s), 0)),
            pl.BlockSpec((rows, B_WIDTH), lambda s: (bwd(s), 0)),
        ],
        out_shape=[out, out],
        scratch_shapes=[pltpu.VMEM((2, nb, B_HEADS, B_DK, B_DV), F32)],
        compiler_params=_params("arbitrary"),
        name="gla",
    )(p, p, p, p, p, p, p, p, wd_f, bd_f, wd_b, bd_b)


def _mix_prepare(u_ref, v_ref, of_ref, ob_ref, g_ref, lng_ref, lnb_ref, ws_ref, bs_ref,
                 og_ref, cat_ref, nb):
    for rb in range(nb):
        rows = slice(rb * CHUNK, (rb + 1) * CHUNK)
        for h in range(A_HEADS):
            cols = slice(h * A_HEAD_DIM, (h + 1) * A_HEAD_DIM)
            v = _gelu_tanh(v_ref[rows, cols].astype(F32))
            mu = jnp.mean(v, axis=-1, keepdims=True)
            vc = v - mu
            var = jnp.mean(vc * vc, axis=-1, keepdims=True)
            vn = vc * lax.rsqrt(var + EPS) * lng_ref[:, cols] + lnb_ref[:, cols]
            mixed = jnp.dot(ws_ref[h], vn.astype(BF16), preferred_element_type=F32)
            mixed = mixed + bs_ref[:, h:h + 1]
            u = _gelu_tanh(u_ref[rows, cols].astype(F32))
            cat_ref[rows, cols] = (u * mixed).astype(BF16)
        for h in range(B_HEADS):
            cols = slice(h * B_DV, (h + 1) * B_DV)
            o = of_ref[rows, cols] + ob_ref[rows, cols]
            y = _rms(o) * og_ref[:, cols]
            gated = y * _silu(g_ref[rows, cols].astype(F32))
            cat_ref[rows, A_WIDTH + h * B_DV:A_WIDTH + (h + 1) * B_DV] = gated.astype(BF16)


def _mix_kernel(*refs, first, nb, group0, n_ctx_groups):
    n_x = 2 if first else 1
    prep_refs = refs[:5]
    x_refs = refs[5:5 + n_x]
    (lng_ref, lnb_ref, ws_ref, bs_ref, mod_ref, og_ref, gpost_ref, wo_ref,
     o_ref, cat0_ref, cat1_ref, y_ref) = refs[5 + n_x:]
    s = pl.program_id(0)

    def prepare(cat_ref):
        _mix_prepare(*prep_refs, lng_ref, lnb_ref, ws_ref, bs_ref, og_ref, cat_ref, nb)

    def project(cat_new, cat_old):
        prepare(cat_new)
        y_ref[...] = jnp.dot(cat_old[...], wo_ref[...], preferred_element_type=F32)

    @pl.when(s == 0)
    def _():
        prepare(cat0_ref)

    @pl.when((s > 0) & (s % 2 == 1))
    def _():
        project(cat1_ref, cat0_ref)

    @pl.when((s > 0) & (s % 2 == 0))
    def _():
        project(cat0_ref, cat1_ref)

    @pl.when(s > 0)
    def _():
        group = s - 1 + group0

        def epilogue(load):
            def residual(rb):
                m = _mod_row(group, rb, nb, n_ctx_groups)
                gain = gpost_ref[...] * mod_ref[m, 2:3, :]
                rows = _chunk_rows(rb)

                def write(cols, val):
                    o_ref[rows, cols] = val
                _norm_residual(lambda cols: y_ref[rows, cols], lambda cols: load(rb, cols),
                               write, gain)
            _for_chunks(nb, residual)
        _with_source(first, group, n_ctx_groups, nb, x_refs, epilogue)


def _mix_call(p, o_f, o_b, x_args, ln_g, ln_b, w_s, b_s, mod, out_g, g_post, w_out, layer,
              first, nb, group0, n_groups, n_ctx_groups):
    d = D_MODEL
    rows = nb * CHUNK
    n_tiles = n_groups - group0
    prep = lambda s: group0 + jnp.minimum(s, n_tiles - 1)
    post = lambda s: group0 + jnp.maximum(s - 1, 0)
    kern = functools.partial(_mix_kernel, first=first, nb=nb, group0=group0,
                             n_ctx_groups=n_ctx_groups)
    return pl.pallas_call(
        kern,
        grid=(n_tiles + 1,),
        in_specs=[
            pl.BlockSpec((rows, A_WIDTH), lambda s: (prep(s), COL_U // A_WIDTH)),
            pl.BlockSpec((rows, A_WIDTH), lambda s: (prep(s), COL_V // A_WIDTH)),
            pl.BlockSpec((rows, B_WIDTH), lambda s: (prep(s), 0)),
            pl.BlockSpec((rows, B_WIDTH), lambda s: (prep(s), 0)),
            pl.BlockSpec((rows, B_WIDTH), lambda s: (prep(s), COL_G // B_WIDTH)),
        ] + _x_specs(first, nb, 1, n_ctx_groups, d, post) + [
            pl.BlockSpec((None, 1, A_WIDTH), lambda s: (layer, 0, 0)),
            pl.BlockSpec((None, 1, A_WIDTH), lambda s: (layer, 0, 0)),
            pl.BlockSpec((None, A_HEADS, CHUNK, CHUNK), lambda s: (layer, 0, 0, 0)),
            pl.BlockSpec((None, CHUNK, A_HEADS), lambda s: (layer, 0, 0)),
            pl.BlockSpec((None, MOD_ROWS, 6, d), lambda s: (layer, 0, 0, 0)),
            pl.BlockSpec((None, 1, B_WIDTH), lambda s: (layer, 0, 0)),
            pl.BlockSpec((None, 1, d), lambda s: (layer, 0, 0)),
            pl.BlockSpec((None, A_WIDTH + B_WIDTH, d), lambda s: (layer, 0, 0),
                         pipeline_mode=pl.Buffered(1)),
        ],
        out_specs=pl.BlockSpec((rows, d), lambda s: (post(s), 0)),
        out_shape=jax.ShapeDtypeStruct((n_groups * rows, d), F32),
        scratch_shapes=[pltpu.VMEM((rows, A_WIDTH + B_WIDTH), BF16),
                        pltpu.VMEM((rows, A_WIDTH + B_WIDTH), BF16),
                        pltpu.VMEM((rows, d), F32)],
        input_output_aliases={} if first else {5: 0},
        compiler_params=_params("arbitrary"),
        name="mix",
    )(p, p, o_f, o_b, p, *x_args, ln_g, ln_b, w_s, b_s, mod, out_g, g_post, w_out)


def _ffn_kernel(x_ref, mod_ref, gpre_ref, gpost_ref, wa_ref, wb_ref, wo_ref, o_ref,
                h_ref, act_ref, *, last, nb, group0, n_ctx_groups):
    j = pl.program_id(1)
    group = pl.program_id(0) + group0

    @pl.when(j == 0)
    def _():
        def body(rb):
            m = _mod_row(group, rb, nb, n_ctx_groups)
            gain = gpre_ref[...] * (1.0 + mod_ref[m, 4:5, :])
            h = _rms(x_ref[_chunk_rows(rb), :]) * gain + mod_ref[m, 3:4, :]
            h_ref[_chunk_rows(rb), :] = h.astype(BF16)
        _for_chunks(nb, body)

    @pl.when(j < FFN_NH)
    def _():
        h = h_ref[...]
        a = jnp.dot(h, wa_ref[...], preferred_element_type=F32)
        b = jnp.dot(h, wb_ref[...], preferred_element_type=F32)
        act_ref[j] = (_silu(a) * b).astype(BF16)

    @pl.when(j == FFN_NH)
    def _():
        for n in range(FFN_NO):
            cols = slice(n * FFN_TN, (n + 1) * FFN_TN)
            acc = jnp.dot(act_ref[0], wo_ref[0:FFN_TH, cols], preferred_element_type=F32)
            for k in range(1, FFN_NH):
                acc += jnp.dot(act_ref[k], wo_ref[k * FFN_TH:(k + 1) * FFN_TH, cols],
                               preferred_element_type=F32)
            if last:
                for b in range(nb):
                    o_ref[b, :, cols] = acc[b * CHUNK:(b + 1) * CHUNK, :]
            else:
                o_ref[:, cols] = acc

        def body(rb):
            rows = _chunk_rows(rb)
            m = _mod_row(group, rb, nb, n_ctx_groups)
            gain = gpost_ref[...] * mod_ref[m, 5:6, :]

            def read_y(cols):
                return o_ref[rb, :, cols] if last else o_ref[rows, cols]

            def write(cols, val):
                if last:
                    o_ref[rb, :, cols] = val
                else:
                    o_ref[rows, cols] = val
            _norm_residual(read_y, lambda cols: x_ref[rows, cols], write, gain)
        _for_chunks(nb, body)


def _ffn_call(xs, mod, g_pre, g_post, w_ffn_in, w_ffn_out, layer, last, nb, group0,
              n_groups, n_ctx_groups):
    d = D_MODEL
    rows = nb * CHUNK
    last_h = FFN_NH - 1
    tile = lambda i: i + group0
    if last:
        out_spec = pl.BlockSpec((nb, CHUNK, d), lambda i, j: (0, i, 0))
        out_shape = jax.ShapeDtypeStruct((nb, (n_groups - group0) * CHUNK, d), F32)
    else:
        out_spec = pl.BlockSpec((rows, d), lambda i, j: (tile(i), 0))
        out_shape = jax.ShapeDtypeStruct((n_groups * rows, d), F32)
    kern = functools.partial(_ffn_kernel, last=last, nb=nb, group0=group0,
                             n_ctx_groups=n_ctx_groups)
    return pl.pallas_call(
        kern,
        grid=(n_groups - group0, FFN_NH + 1),
        in_specs=[
            pl.BlockSpec((rows, d), lambda i, j: (tile(i), 0)),
            pl.BlockSpec((None, MOD_ROWS, 6, d), lambda i, j: (layer, 0, 0, 0)),
            pl.BlockSpec((None, 1, d), lambda i, j: (layer, 0, 0)),
            pl.BlockSpec((None, 1, d), lambda i, j: (layer, 0, 0)),
            pl.BlockSpec((None, d, FFN_TH), lambda i, j: (layer, 0, jnp.minimum(j, last_h))),
            pl.BlockSpec((None, d, FFN_TH),
                         lambda i, j: (layer, 0, FFN_NH + jnp.minimum(j, last_h))),
            pl.BlockSpec((None, FFN_HIDDEN, d), lambda i, j: (layer, 0, 0),
                         pipeline_mode=pl.Buffered(1)),
        ],
        out_specs=out_spec,
        out_shape=out_shape,
        scratch_shapes=[pltpu.VMEM((rows, d), BF16),
                        pltpu.VMEM((FFN_NH, rows, FFN_TH), BF16)],
        input_output_aliases={} if last else {0: 0},
        compiler_params=_params("arbitrary", "arbitrary", vmem_limit=FFN_VMEM_LIMIT),
        name="ffn",
    )(xs, mod, g_pre, g_post, w_ffn_in, w_ffn_in, w_ffn_out)


def kernel(x, c, ctx, c_ctx, w_mod, b_mod, g_pre_mix, g_post_mix, g_pre_ffn, g_post_ffn, w_in, gmlp_ln_g, gmlp_ln_b, gmlp_ws, gmlp_bs, gla_wd2_fwd, gla_bd_fwd, gla_wd2_bwd, gla_bd_bwd, gla_out_g, w_out, w_ffn_in, w_ffn_out):
    nb, seq, d = x.shape
    ctx_len = ctx.shape[1]
    n_layers = w_mod.shape[0]
    assert d == D_MODEL and nb + 1 <= MOD_ROWS
    assert seq % CHUNK == 0 and ctx_len % CHUNK == 0
    n_ctx_groups = ctx_len // CHUNK
    n_groups = n_ctx_groups + seq // CHUNK

    act = jnp.zeros((MOD_ROWS, d), F32).at[:nb].set(c).at[nb].set(c_ctx)
    mod = _mod_call(act, w_mod, b_mod).reshape(n_layers, MOD_ROWS, 6, d)

    row = lambda t: t.reshape(n_layers, 1, t.shape[-1])
    w_in_b = jnp.pad(w_in.astype(BF16), ((0, 0), (0, 0), (0, P_PAD - P_COLS)))
    w_out_b = w_out.astype(BF16)
    w_ffn_in_b = w_ffn_in.astype(BF16)
    w_ffn_out_b = w_ffn_out.astype(BF16)
    w_s_b = gmlp_ws.astype(BF16)
    b_s_t = jnp.swapaxes(gmlp_bs, 1, 2)
    wd_f = jnp.pad(gla_wd2_fwd, ((0, 0), (0, D_BLOCK - GATE_RANK), (0, 0))).astype(BF16)
    wd_b = jnp.pad(gla_wd2_bwd, ((0, 0), (GATE_RANK, D_BLOCK - 2 * GATE_RANK), (0, 0))).astype(BF16)

    x_args = (x, ctx)
    for l in range(n_layers):
        first = l == 0
        last = l == n_layers - 1
        group0 = n_ctx_groups if last else 0
        p = _inproj_call(x_args, mod, row(g_pre_mix), w_in_b, l, first, nb, n_groups,
                         n_ctx_groups)
        o_f, o_b = _gla_call(p, wd_f, row(gla_bd_fwd), wd_b, row(gla_bd_bwd), l, nb,
                             n_groups, n_ctx_groups)
        xs = _mix_call(p, o_f, o_b, x_args, row(gmlp_ln_g), row(gmlp_ln_b), w_s_b, b_s_t, mod,
                       row(gla_out_g), row(g_post_mix), w_out_b, l, first, nb, group0,
                       n_groups, n_ctx_groups)
        xs = _ffn_call(xs, mod, row(g_pre_ffn), row(g_post_ffn), w_ffn_in_b, w_ffn_out_b,
                       l, last, nb, group0, n_groups, n_ctx_groups)
        x_args = (xs,)
    return xs
```

```python
import functools

import jax
import jax.numpy as jnp
from jax import lax
from jax.experimental import pallas as pl
from jax.experimental.pallas import tpu as pltpu

F32 = jnp.float32
BF16 = jnp.bfloat16

D_MODEL = 2048
CHUNK = 128
A_WIDTH = 1024
A_HEADS = 8
A_HEAD_DIM = A_WIDTH // A_HEADS
B_WIDTH = 1024
B_HEADS = 4
B_DV = B_WIDTH // B_HEADS
B_DK = B_DV // 2
B_KEY_W = B_HEADS * B_DK
GATE_RANK = 16
GATE_TAU = 16.0
LOG2_Q_SCALE = -3.5
FFN_HIDDEN = 5632
EPS = 1e-6

COL_U = 0
COL_V = A_WIDTH
COL_Q = 2 * A_WIDTH
COL_K = COL_Q + B_KEY_W
COL_VV = COL_K + B_KEY_W
COL_G = COL_VV + B_WIDTH
COL_D = COL_G + B_WIDTH
P_COLS = COL_D + 2 * GATE_RANK
MXU_WIDTH = 256
P_PAD = -(-P_COLS // MXU_WIDTH) * MXU_WIDTH
D_BLOCK = 128

FFN_TH = 512
FFN_TN = 512
FFN_NH = FFN_HIDDEN // FFN_TH
FFN_NO = D_MODEL // FFN_TN
MOD_TN = 1024
MOD_ROWS = 8
NORM_COLS = 512

VMEM_BYTES = 64 * 1024 * 1024
VMEM_LIMIT = VMEM_BYTES - 8 * 1024 * 1024
FFN_VMEM_LIMIT = VMEM_BYTES - 4 * 1024 * 1024


def _params(*sem, vmem_limit=VMEM_LIMIT):
    return pltpu.CompilerParams(dimension_semantics=sem, vmem_limit_bytes=vmem_limit)


def _silu(x):
    return x * jax.nn.sigmoid(x)


def _gelu_tanh(x):
    k = 0.7978845608028654
    return x * (0.5 + 0.5 * jnp.tanh(x * (k + (k * 0.044715) * (x * x))))


def _rms(x):
    return x * lax.rsqrt(jnp.mean(x * x, axis=-1, keepdims=True) + EPS)


def _chunk_rows(rb):
    return pl.ds(pl.multiple_of(rb * CHUNK, CHUNK), CHUNK)


def _for_chunks(n, body):
    def step(rb, carry):
        body(rb)
        return carry
    lax.fori_loop(0, n, step, 0)


def _mod_row(group, rb, nb, n_ctx_groups):
    g = group + rb // nb
    return jnp.where(g < n_ctx_groups, nb, rb % nb)


def _load_canonical(ref):
    return lambda rb, cols=slice(None): ref[_chunk_rows(rb), cols]


def _load_sample_major(ref, nb):
    return lambda rb, cols=slice(None): ref[rb % nb, _chunk_rows(rb // nb), cols]


def _norm_residual(read_y, read_x, write, gain):
    blocks = [slice(c, c + NORM_COLS) for c in range(0, D_MODEL, NORM_COLS)]
    ss = None
    for cols in blocks:
        y = read_y(cols)
        part = jnp.sum(y * y, axis=-1, keepdims=True)
        ss = part if ss is None else ss + part
    inv = lax.rsqrt(ss * (1.0 / D_MODEL) + EPS)
    for cols in blocks:
        write(cols, read_x(cols) + read_y(cols) * inv * gain[:, cols])


def _mod_kernel(act_ref, w_ref, b_ref, o_ref):
    a = _silu(act_ref[...])
    o_ref[...] = jnp.dot(a.astype(BF16), w_ref[...].astype(BF16),
                         preferred_element_type=F32) + b_ref[...]


def _mod_call(act, w_mod, b_mod):
    n_layers, d, n = w_mod.shape
    return pl.pallas_call(
        _mod_kernel,
        grid=(n_layers, n // MOD_TN),
        in_specs=[
            pl.BlockSpec((MOD_ROWS, d), lambda l, j: (0, 0)),
            pl.BlockSpec((None, d, MOD_TN), lambda l, j: (l, 0, j)),
            pl.BlockSpec((None, 1, MOD_TN), lambda l, j: (l, 0, j)),
        ],
        out_specs=pl.BlockSpec((None, MOD_ROWS, MOD_TN), lambda l, j: (l, 0, j)),
        out_shape=jax.ShapeDtypeStruct((n_layers, MOD_ROWS, n), F32),
        compiler_params=_params("arbitrary", "arbitrary"),
        name="mod",
    )(act, w_mod, b_mod.reshape(n_layers, 1, n))


def _x_specs(first, nb, groups, n_ctx_tiles, d, tile_of):
    if not first:
        return [pl.BlockSpec((groups * nb * CHUNK, d), lambda *ix: (tile_of(*ix), 0))]
    blk = (nb, groups * CHUNK, d)
    return [
        pl.BlockSpec(blk, lambda *ix: (0, jnp.maximum(tile_of(*ix) - n_ctx_tiles, 0), 0)),
        pl.BlockSpec(blk, lambda *ix: (0, jnp.minimum(tile_of(*ix), n_ctx_tiles - 1), 0),
                     pipeline_mode=pl.Buffered(1)),
    ]


def _with_source(first, tile, n_ctx_tiles, nb, x_refs, fn):
    if not first:
        fn(_load_canonical(x_refs[0]))
        return

    @pl.when(tile >= n_ctx_tiles)
    def _():
        fn(_load_sample_major(x_refs[0], nb))

    @pl.when(tile < n_ctx_tiles)
    def _():
        fn(_load_sample_major(x_refs[1], nb))


def _inproj_kernel(*refs, first, nb, n_groups, n_ctx_groups):
    n_x = 2 if first else 1
    x_refs = refs[:n_x]
    mod_ref, g_ref, w_ref, wd_ref, o_ref, h0_ref, h1_ref = refs[n_x:]
    s = pl.program_id(0)
    is_ctx = jnp.minimum(s, n_groups - 1) < n_ctx_groups

    def load(rb):
        if not first:
            return x_refs[0][rb * CHUNK:(rb + 1) * CHUNK, :]
        return jnp.where(is_ctx, x_refs[1][rb], x_refs[0][rb])

    def prepare(h_ref):
        for rb in range(nb):
            m = jnp.where(is_ctx, nb, rb)
            gain = g_ref[...] * (1.0 + mod_ref[m, 1:2, :])
            h = _rms(load(rb)) * gain + mod_ref[m, 0:1, :]
            h_ref[rb * CHUNK:(rb + 1) * CHUNK, :] = h.astype(BF16)

    def project(h_new, h_old):
        prepare(h_new)
        h = h_old[...]
        o_ref[:, 0:COL_D] = jnp.dot(h, w_ref[...], preferred_element_type=F32).astype(BF16)
        o_ref[:, COL_D:P_PAD] = jnp.dot(h, wd_ref[...], preferred_element_type=F32).astype(BF16)

    @pl.when(s == 0)
    def _():
        prepare(h0_ref)

    @pl.when((s > 0) & (s % 2 == 1))
    def _():
        project(h1_ref, h0_ref)

    @pl.when((s > 0) & (s % 2 == 0))
    def _():
        project(h0_ref, h1_ref)


def _inproj_call(x_args, mod, g_pre, w_in, w_in_d, layer, first, nb, n_groups, n_ctx_groups):
    d = D_MODEL
    rows = nb * CHUNK
    prep = lambda s: jnp.minimum(s, n_groups - 1)
    post = lambda s: jnp.maximum(s - 1, 0)
    kern = functools.partial(_inproj_kernel, first=first, nb=nb, n_groups=n_groups,
                             n_ctx_groups=n_ctx_groups)
    return pl.pallas_call(
        kern,
        grid=(n_groups + 1,),
        in_specs=_x_specs(first, nb, 1, n_ctx_groups, d, prep) + [
            pl.BlockSpec((None, MOD_ROWS, 6, d), lambda s: (layer, 0, 0, 0)),
            pl.BlockSpec((None, 1, d), lambda s: (layer, 0, 0)),
            pl.BlockSpec((None, d, COL_D), lambda s: (layer, 0, 0),
                         pipeline_mode=pl.Buffered(1)),
            pl.BlockSpec((None, d, P_PAD - COL_D), lambda s: (layer, 0, 0),
                         pipeline_mode=pl.Buffered(1)),
        ],
        out_specs=pl.BlockSpec((rows, P_PAD), lambda s: (post(s), 0)),
        out_shape=jax.ShapeDtypeStruct((n_groups * rows, P_PAD), BF16),
        scratch_shapes=[pltpu.VMEM((rows, d), BF16), pltpu.VMEM((rows, d), BF16)],
        compiler_params=_params("arbitrary"),
        name="inproj",
    )(*x_args, mod, g_pre, w_in, w_in_d)


LOG2_E = 1.4426950408889634


def _log2_sigmoid(z):
    return jnp.minimum(z, 0.0) * LOG2_E - jnp.log2(1.0 + jnp.exp2(jnp.abs(z) * -LOG2_E))


def _cumsum_rows(tri, la):
    hi = la.astype(BF16)
    lo = (la - hi.astype(F32)).astype(BF16)
    return jnp.dot(jnp.concatenate([tri, tri], axis=1), jnp.concatenate([hi, lo], axis=0),
                   preferred_element_type=F32)


def _gla_direction(q_ref, k_ref, v_ref, d_ref, wd_ref, bd_ref, s_ref, o_ref, reverse, nb):
    row = lax.broadcasted_iota(jnp.int32, (CHUNK, CHUNK), 0)
    col = lax.broadcasted_iota(jnp.int32, (CHUNK, CHUNK), 1)
    causal = (col >= row) if reverse else (col <= row)
    eye = col == row
    tri = jnp.where(causal, 1.0, 0.0).astype(BF16)
    end = 0 if reverse else CHUNK - 1
    mid = CHUNK // 2

    z = jnp.dot(d_ref[...], wd_ref[...], preferred_element_type=F32) + bd_ref[...]
    la_all = _log2_sigmoid(z) * (1.0 / GATE_TAU)

    for b in range(nb):
        rows = slice(b * CHUNK, (b + 1) * CHUNK)
        cum = _cumsum_rows(tri, la_all[rows])
        c_end = cum[end:end + 1, :]
        c_mid = cum[mid:mid + 1, :]
        e_mid = jnp.exp2(c_mid)
        e_end_mid = jnp.exp2(c_end - c_mid)
        e_end = jnp.exp2(c_end)
        c_mid_q = c_mid - LOG2_Q_SCALE
        for h in range(B_HEADS):
            kc = slice(h * B_DK, (h + 1) * B_DK)
            vc = slice(h * B_DV, (h + 1) * B_DV)
            q = q_ref[rows, kc].astype(F32)
            k = k_ref[rows, kc].astype(F32)
            v = v_ref[rows, vc]
            q_t = q * jnp.exp2(cum[:, kc] - c_mid_q[:, kc])
            k_t = k * jnp.exp2(c_mid[:, kc] - cum[:, kc])
            q_in = (q_t * e_mid[:, kc]).astype(BF16)
            k_d = k_t * e_end_mid[:, kc]
            scores = lax.dot_general(q_t.astype(BF16), k_t.astype(BF16),
                                     (((1,), (1,)), ((), ())), preferred_element_type=F32)
            scores = jnp.where(causal, scores, 0.0).astype(BF16)
            s = s_ref[b, h]
            o = jnp.dot(scores, v, preferred_element_type=F32)
            o = o + jnp.dot(q_in, s.astype(BF16), preferred_element_type=F32)
            o_ref[rows, vc] = o
            decay = jnp.broadcast_to(e_end[:, kc], (B_DK, B_DK))
            decay_col = jnp.sum(jnp.where(eye, decay, 0.0), axis=-1, keepdims=True)
            s_ref[b, h] = decay_col * s + jnp.dot(k_d.T.astype(BF16), v,
                                                  preferred_element_type=F32)


def _gla_kernel(qf_ref, kf_ref, vf_ref, df_ref, qb_ref, kb_ref, vb_ref, db_ref,
                wdf_ref, bdf_ref, wdb_ref, bdb_ref, of_ref, ob_ref, s_ref, *, nb):
    @pl.when(pl.program_id(0) == 0)
    def _():
        s_ref[...] = jnp.zeros_like(s_ref)

    _gla_direction(qf_ref, kf_ref, vf_ref, df_ref, wdf_ref, bdf_ref, s_ref.at[0], of_ref,
                   False, nb)
    _gla_direction(qb_ref, kb_ref, vb_ref, db_ref, wdb_ref, bdb_ref, s_ref.at[1], ob_ref,
                   True, nb)


def _gla_call(p, wd_f, bd_f, wd_b, bd_b, layer, nb, n_groups, n_ctx_groups):
    rows = nb * CHUNK

    def fwd(s):
        return s

    def bwd(s):
        return jnp.where(s < n_ctx_groups, n_ctx_groups - 1 - s, n_groups + n_ctx_groups - 1 - s)

    def chunk_specs(order):
        return [
            pl.BlockSpec((rows, B_KEY_W), lambda s: (order(s), COL_Q // B_KEY_W)),
            pl.BlockSpec((rows, B_KEY_W), lambda s: (order(s), COL_K // B_KEY_W)),
            pl.BlockSpec((rows, B_WIDTH), lambda s: (order(s), COL_VV // B_WIDTH)),
            pl.BlockSpec((rows, D_BLOCK), lambda s: (order(s), COL_D // D_BLOCK)),
        ]

    def weight_specs():
        return [
            pl.BlockSpec((None, D_BLOCK, B_KEY_W), lambda s: (layer, 0, 0)),
            pl.BlockSpec((None, 1, B_KEY_W), lambda s: (layer, 0, 0)),
        ]

    out = jax.ShapeDtypeStruct((n_groups * rows, B_WIDTH), F32)
    return pl.pallas_call(
        functools.partial(_gla_kernel, nb=nb),
        grid=(n_groups,),
        in_specs=chunk_specs(fwd) + chunk_specs(bwd) + weight_specs() + weight_specs(),
        out_specs=[
            pl.BlockSpec((rows, B_WIDTH), lambda s: (fwd(s), 0)),
            pl.BlockSpec((rows, B_WIDTH), lambda s: (bwd(s), 0)),
        ],
        out_shape=[out, out],
        scratch_shapes=[pltpu.VMEM((2, nb, B_HEADS, B_DK, B_DV), F32)],
        compiler_params=_params("arbitrary"),
        name="gla",
    )(p, p, p, p, p, p, p, p, wd_f, bd_f, wd_b, bd_b)


def _mix_prepare(u_ref, v_ref, of_ref, ob_ref, g_ref, lng_ref, lnb_ref, ws_ref, bs_ref,
                 og_ref, cat_ref, nb):
    for rb in range(nb):
        rows = slice(rb * CHUNK, (rb + 1) * CHUNK)
        for h in range(A_HEADS):
            cols = slice(h * A_HEAD_DIM, (h + 1) * A_HEAD_DIM)
            v = _gelu_tanh(v_ref[rows, cols].astype(F32))
            mu = jnp.mean(v, axis=-1, keepdims=True)
            vc = v - mu
            var = jnp.mean(vc * vc, axis=-1, keepdims=True)
            vn = vc * lax.rsqrt(var + EPS) * lng_ref[:, cols] + lnb_ref[:, cols]
            mixed = jnp.dot(ws_ref[h], vn.astype(BF16), preferred_element_type=F32)
            mixed = mixed + bs_ref[:, h:h + 1]
            u = _gelu_tanh(u_ref[rows, cols].astype(F32))
            cat_ref[rows, cols] = (u * mixed).astype(BF16)
        for h in range(B_HEADS):
            cols = slice(h * B_DV, (h + 1) * B_DV)
            o = of_ref[rows, cols] + ob_ref[rows, cols]
            y = _rms(o) * og_ref[:, cols]
            gated = y * _silu(g_ref[rows, cols].astype(F32))
            cat_ref[rows, A_WIDTH + h * B_DV:A_WIDTH + (h + 1) * B_DV] = gated.astype(BF16)


def _mix_kernel(*refs, first, nb, group0, n_ctx_groups):
    n_x = 2 if first else 1
    prep_refs = refs[:5]
    x_refs = refs[5:5 + n_x]
    (lng_ref, lnb_ref, ws_ref, bs_ref, mod_ref, og_ref, gpost_ref, wo_ref,
     o_ref, cat0_ref, cat1_ref, y_ref) = refs[5 + n_x:]
    s = pl.program_id(0)

    def prepare(cat_ref):
        _mix_prepare(*prep_refs, lng_ref, lnb_ref, ws_ref, bs_ref, og_ref, cat_ref, nb)

    def project(cat_new, cat_old):
        prepare(cat_new)
        y_ref[...] = jnp.dot(cat_old[...], wo_ref[...], preferred_element_type=F32)

    @pl.when(s == 0)
    def _():
        prepare(cat0_ref)

    @pl.when((s > 0) & (s % 2 == 1))
    def _():
        project(cat1_ref, cat0_ref)

    @pl.when((s > 0) & (s % 2 == 0))
    def _():
        project(cat0_ref, cat1_ref)

    @pl.when(s > 0)
    def _():
        group = s - 1 + group0

        def epilogue(load):
            def residual(rb):
                m = _mod_row(group, rb, nb, n_ctx_groups)
                gain = gpost_ref[...] * mod_ref[m, 2:3, :]
                rows = _chunk_rows(rb)

                def write(cols, val):
                    o_ref[rows, cols] = val
                _norm_residual(lambda cols: y_ref[rows, cols], lambda cols: load(rb, cols),
                               write, gain)
            _for_chunks(nb, residual)
        _with_source(first, group, n_ctx_groups, nb, x_refs, epilogue)


def _mix_call(p, o_f, o_b, x_args, ln_g, ln_b, w_s, b_s, mod, out_g, g_post, w_out, layer,
              first, nb, group0, n_groups, n_ctx_groups):
    d = D_MODEL
    rows = nb * CHUNK
    n_tiles = n_groups - group0
    prep = lambda s: group0 + jnp.minimum(s, n_tiles - 1)
    post = lambda s: group0 + jnp.maximum(s - 1, 0)
    kern = functools.partial(_mix_kernel, first=first, nb=nb, group0=group0,
                             n_ctx_groups=n_ctx_groups)
    return pl.pallas_call(
        kern,
        grid=(n_tiles + 1,),
        in_specs=[
            pl.BlockSpec((rows, A_WIDTH), lambda s: (prep(s), COL_U // A_WIDTH)),
            pl.BlockSpec((rows, A_WIDTH), lambda s: (prep(s), COL_V // A_WIDTH)),
            pl.BlockSpec((rows, B_WIDTH), lambda s: (prep(s), 0)),
            pl.BlockSpec((rows, B_WIDTH), lambda s: (prep(s), 0)),
            pl.BlockSpec((rows, B_WIDTH), lambda s: (prep(s), COL_G // B_WIDTH)),
        ] + _x_specs(first, nb, 1, n_ctx_groups, d, post) + [
            pl.BlockSpec((None, 1, A_WIDTH), lambda s: (layer, 0, 0)),
            pl.BlockSpec((None, 1, A_WIDTH), lambda s: (layer, 0, 0)),
            pl.BlockSpec((None, A_HEADS, CHUNK, CHUNK), lambda s: (layer, 0, 0, 0)),
            pl.BlockSpec((None, CHUNK, A_HEADS), lambda s: (layer, 0, 0)),
            pl.BlockSpec((None, MOD_ROWS, 6, d), lambda s: (layer, 0, 0, 0)),
            pl.BlockSpec((None, 1, B_WIDTH), lambda s: (layer, 0, 0)),
            pl.BlockSpec((None, 1, d), lambda s: (layer, 0, 0)),
            pl.BlockSpec((None, A_WIDTH + B_WIDTH, d), lambda s: (layer, 0, 0),
                         pipeline_mode=pl.Buffered(1)),
        ],
        out_specs=pl.BlockSpec((rows, d), lambda s: (post(s), 0)),
        out_shape=jax.ShapeDtypeStruct((n_groups * rows, d), F32),
        scratch_shapes=[pltpu.VMEM((rows, A_WIDTH + B_WIDTH), BF16),
                        pltpu.VMEM((rows, A_WIDTH + B_WIDTH), BF16),
                        pltpu.VMEM((rows, d), F32)],
        input_output_aliases={} if first else {5: 0},
        compiler_params=_params("arbitrary"),
        name="mix",
    )(p, p, o_f, o_b, p, *x_args, ln_g, ln_b, w_s, b_s, mod, out_g, g_post, w_out)


def _ffn_kernel(x_ref, mod_ref, gpre_ref, gpost_ref, wa_ref, wb_ref, wo_ref, o_ref,
                h_ref, act_ref, *, last, nb, group0, n_ctx_groups):
    j = pl.program_id(1)
    group = pl.program_id(0) + group0

    @pl.when(j == 0)
    def _():
        def body(rb):
            m = _mod_row(group, rb, nb, n_ctx_groups)
            gain = gpre_ref[...] * (1.0 + mod_ref[m, 4:5, :])
            h = _rms(x_ref[_chunk_rows(rb), :]) * gain + mod_ref[m, 3:4, :]
            h_ref[_chunk_rows(rb), :] = h.astype(BF16)
        _for_chunks(nb, body)

    @pl.when(j < FFN_NH)
    def _():
        h = h_ref[...]
        a = jnp.dot(h, wa_ref[...], preferred_element_type=F32)
        b = jnp.dot(h, wb_ref[...], preferred_element_type=F32)
        act_ref[j] = (_silu(a) * b).astype(BF16)

    @pl.when(j == FFN_NH)
    def _():
        for n in range(FFN_NO):
            cols = slice(n * FFN_TN, (n + 1) * FFN_TN)
            acc = jnp.dot(act_ref[0], wo_ref[0:FFN_TH, cols], preferred_element_type=F32)
            for k in range(1, FFN_NH):
                acc += jnp.dot(act_ref[k], wo_ref[k * FFN_TH:(k + 1) * FFN_TH, cols],
                               preferred_element_type=F32)
            if last:
                for b in range(nb):
                    o_ref[b, :, cols] = acc[b * CHUNK:(b + 1) * CHUNK, :]
            else:
                o_ref[:, cols] = acc

        def body(rb):
            rows = _chunk_rows(rb)
            m = _mod_row(group, rb, nb, n_ctx_groups)
            gain = gpost_ref[...] * mod_ref[m, 5:6, :]

            def read_y(cols):
                return o_ref[rb, :, cols] if last else o_ref[rows, cols]

            def write(cols, val):
                if last:
                    o_ref[rb, :, cols] = val
                else:
                    o_ref[rows, cols] = val
            _norm_residual(read_y, lambda cols: x_ref[rows, cols], write, gain)
        _for_chunks(nb, body)


def _ffn_call(xs, mod, g_pre, g_post, w_ffn_in, w_ffn_out, layer, last, nb, group0,
              n_groups, n_ctx_groups):
    d = D_MODEL
    rows = nb * CHUNK
    last_h = FFN_NH - 1
    tile = lambda i: i + group0
    if last:
        out_spec = pl.BlockSpec((nb, CHUNK, d), lambda i, j: (0, i, 0))
        out_shape = jax.ShapeDtypeStruct((nb, (n_groups - group0) * CHUNK, d), F32)
    else:
        out_spec = pl.BlockSpec((rows, d), lambda i, j: (tile(i), 0))
        out_shape = jax.ShapeDtypeStruct((n_groups * rows, d), F32)
    kern = functools.partial(_ffn_kernel, last=last, nb=nb, group0=group0,
                             n_ctx_groups=n_ctx_groups)
    return pl.pallas_call(
        kern,
        grid=(n_groups - group0, FFN_NH + 1),
        in_specs=[
            pl.BlockSpec((rows, d), lambda i, j: (tile(i), 0)),
            pl.BlockSpec((None, MOD_ROWS, 6, d), lambda i, j: (layer, 0, 0, 0)),
            pl.BlockSpec((None, 1, d), lambda i, j: (layer, 0, 0)),
            pl.BlockSpec((None, 1, d), lambda i, j: (layer, 0, 0)),
            pl.BlockSpec((None, d, FFN_TH), lambda i, j: (layer, 0, jnp.minimum(j, last_h))),
            pl.BlockSpec((None, d, FFN_TH),
                         lambda i, j: (layer, 0, FFN_NH + jnp.minimum(j, last_h))),
            pl.BlockSpec((None, FFN_HIDDEN, d), lambda i, j: (layer, 0, 0),
                         pipeline_mode=pl.Buffered(1)),
        ],
        out_specs=out_spec,
        out_shape=out_shape,
        scratch_shapes=[pltpu.VMEM((rows, d), BF16),
                        pltpu.VMEM((FFN_NH, rows, FFN_TH), BF16)],
        input_output_aliases={} if last else {0: 0},
        compiler_params=_params("arbitrary", "arbitrary", vmem_limit=FFN_VMEM_LIMIT),
        name="ffn",
    )(xs, mod, g_pre, g_post, w_ffn_in, w_ffn_in, w_ffn_out)


def kernel(x, c, ctx, c_ctx, w_mod, b_mod, g_pre_mix, g_post_mix, g_pre_ffn, g_post_ffn, w_in, gmlp_ln_g, gmlp_ln_b, gmlp_ws, gmlp_bs, gla_wd2_fwd, gla_bd_fwd, gla_wd2_bwd, gla_bd_bwd, gla_out_g, w_out, w_ffn_in, w_ffn_out):
    nb, seq, d = x.shape
    ctx_len = ctx.shape[1]
    n_layers = w_mod.shape[0]
    assert d == D_MODEL and nb + 1 <= MOD_ROWS
    assert seq % CHUNK == 0 and ctx_len % CHUNK == 0
    n_ctx_groups = ctx_len // CHUNK
    n_groups = n_ctx_groups + seq // CHUNK

    act = jnp.zeros((MOD_ROWS, d), F32).at[:nb].set(c).at[nb].set(c_ctx)
    mod = _mod_call(act, w_mod, b_mod).reshape(n_layers, MOD_ROWS, 6, d)

    row = lambda t: t.reshape(n_layers, 1, t.shape[-1])
    w_in_b = w_in[:, :, :COL_D].astype(BF16)
    w_in_d = jnp.pad(w_in[:, :, COL_D:].astype(BF16), ((0, 0), (0, 0), (0, P_PAD - P_COLS)))
    w_out_b = w_out.astype(BF16)
    w_ffn_in_b = w_ffn_in.astype(BF16)
    w_ffn_out_b = w_ffn_out.astype(BF16)
    w_s_b = gmlp_ws.astype(BF16)
    b_s_t = jnp.swapaxes(gmlp_bs, 1, 2)
    wd_f = jnp.pad(gla_wd2_fwd, ((0, 0), (0, D_BLOCK - GATE_RANK), (0, 0))).astype(BF16)
    wd_b = jnp.pad(gla_wd2_bwd, ((0, 0), (GATE_RANK, D_BLOCK - 2 * GATE_RANK), (0, 0))).astype(BF16)

    x_args = (x, ctx)
    for l in range(n_layers):
        first = l == 0
        last = l == n_layers - 1
        group0 = n_ctx_groups if last else 0
        p = _inproj_call(x_args, mod, row(g_pre_mix), w_in_b, w_in_d, l, first, nb, n_groups,
                         n_ctx_groups)
        o_f, o_b = _gla_call(p, wd_f, row(gla_bd_fwd), wd_b, row(gla_bd_bwd), l, nb,
                             n_groups, n_ctx_groups)
        xs = _mix_call(p, o_f, o_b, x_args, row(gmlp_ln_g), row(gmlp_ln_b), w_s_b, b_s_t, mod,
                       row(gla_out_g), row(g_post_mix), w_out_b, l, first, nb, group0,
                       n_groups, n_ctx_groups)
        xs = _ffn_call(xs, mod, row(g_pre_ffn), row(g_post_ffn), w_ffn_in_b, w_ffn_out_b,
                       l, last, nb, group0, n_groups, n_ctx_groups)
        x_args = (xs,)
    return xs
```

```python
import functools

import jax
import jax.numpy as jnp
from jax import lax
from jax.experimental import pallas as pl
from jax.experimental.pallas import tpu as pltpu

F32 = jnp.float32
BF16 = jnp.bfloat16

D_MODEL = 2048
CHUNK = 128
A_WIDTH = 1024
A_HEADS = 8
A_HEAD_DIM = A_WIDTH // A_HEADS
B_WIDTH = 1024
B_HEADS = 4
B_DV = B_WIDTH // B_HEADS
B_DK = B_DV // 2
B_KEY_W = B_HEADS * B_DK
GATE_RANK = 16
GATE_TAU = 16.0
LOG2_Q_SCALE = -3.5
FFN_HIDDEN = 5632
EPS = 1e-6

COL_U = 0
COL_V = A_WIDTH
COL_Q = 2 * A_WIDTH
COL_K = COL_Q + B_KEY_W
COL_VV = COL_K + B_KEY_W
COL_G = COL_VV + B_WIDTH
COL_D = COL_G + B_WIDTH
P_COLS = COL_D + 2 * GATE_RANK
D_BLOCK = 128
P_PAD = COL_D + D_BLOCK

FFN_TH = 512
FFN_TN = 512
FFN_NH = FFN_HIDDEN // FFN_TH
FFN_NO = D_MODEL // FFN_TN
MOD_TN = 1024
MOD_ROWS = 8
NORM_COLS = 512

VMEM_BYTES = 64 * 1024 * 1024
VMEM_LIMIT = VMEM_BYTES - 8 * 1024 * 1024
FFN_VMEM_LIMIT = VMEM_BYTES - 4 * 1024 * 1024


def _params(*sem, vmem_limit=VMEM_LIMIT):
    return pltpu.CompilerParams(dimension_semantics=sem, vmem_limit_bytes=vmem_limit)


def _silu(x):
    return x * jax.nn.sigmoid(x)


def _gelu_tanh(x):
    k = 0.7978845608028654
    return x * (0.5 + 0.5 * jnp.tanh(x * (k + (k * 0.044715) * (x * x))))


def _rms(x):
    return x * lax.rsqrt(jnp.mean(x * x, axis=-1, keepdims=True) + EPS)


def _chunk_rows(rb):
    return pl.ds(pl.multiple_of(rb * CHUNK, CHUNK), CHUNK)


def _for_chunks(n, body):
    def step(rb, carry):
        body(rb)
        return carry
    lax.fori_loop(0, n, step, 0)


def _mod_row(group, rb, nb, n_ctx_groups):
    g = group + rb // nb
    return jnp.where(g < n_ctx_groups, nb, rb % nb)


def _load_canonical(ref):
    return lambda rb, cols=slice(None): ref[_chunk_rows(rb), cols]


def _load_sample_major(ref, nb):
    return lambda rb, cols=slice(None): ref[rb % nb, _chunk_rows(rb // nb), cols]


def _norm_residual(read_y, read_x, write, gain):
    blocks = [slice(c, c + NORM_COLS) for c in range(0, D_MODEL, NORM_COLS)]
    ss = None
    for cols in blocks:
        y = read_y(cols)
        part = jnp.sum(y * y, axis=-1, keepdims=True)
        ss = part if ss is None else ss + part
    inv = lax.rsqrt(ss * (1.0 / D_MODEL) + EPS)
    for cols in blocks:
        write(cols, read_x(cols) + read_y(cols) * inv * gain[:, cols])


def _mod_kernel(act_ref, w_ref, b_ref, o_ref):
    a = _silu(act_ref[...])
    o_ref[...] = jnp.dot(a.astype(BF16), w_ref[...].astype(BF16),
                         preferred_element_type=F32) + b_ref[...]


def _mod_call(act, w_mod, b_mod):
    n_layers, d, n = w_mod.shape
    return pl.pallas_call(
        _mod_kernel,
        grid=(n_layers, n // MOD_TN),
        in_specs=[
            pl.BlockSpec((MOD_ROWS, d), lambda l, j: (0, 0)),
            pl.BlockSpec((None, d, MOD_TN), lambda l, j: (l, 0, j)),
            pl.BlockSpec((None, 1, MOD_TN), lambda l, j: (l, 0, j)),
        ],
        out_specs=pl.BlockSpec((None, MOD_ROWS, MOD_TN), lambda l, j: (l, 0, j)),
        out_shape=jax.ShapeDtypeStruct((n_layers, MOD_ROWS, n), F32),
        compiler_params=_params("arbitrary", "arbitrary"),
        name="mod",
    )(act, w_mod, b_mod.reshape(n_layers, 1, n))


def _x_specs(first, nb, groups, n_ctx_tiles, d, tile_of):
    if not first:
        return [pl.BlockSpec((groups * nb * CHUNK, d), lambda *ix: (tile_of(*ix), 0))]
    blk = (nb, groups * CHUNK, d)
    return [
        pl.BlockSpec(blk, lambda *ix: (0, jnp.maximum(tile_of(*ix) - n_ctx_tiles, 0), 0)),
        pl.BlockSpec(blk, lambda *ix: (0, jnp.minimum(tile_of(*ix), n_ctx_tiles - 1), 0),
                     pipeline_mode=pl.Buffered(1)),
    ]


def _with_source(first, tile, n_ctx_tiles, nb, x_refs, fn):
    if not first:
        fn(_load_canonical(x_refs[0]))
        return

    @pl.when(tile >= n_ctx_tiles)
    def _():
        fn(_load_sample_major(x_refs[0], nb))

    @pl.when(tile < n_ctx_tiles)
    def _():
        fn(_load_sample_major(x_refs[1], nb))


def _inproj_kernel(*refs, first, nb, n_groups, n_ctx_groups):
    n_x = 2 if first else 1
    x_refs = refs[:n_x]
    mod_ref, g_ref, w_ref, wd_ref, o_ref, h0_ref, h1_ref = refs[n_x:]
    s = pl.program_id(0)
    is_ctx = jnp.minimum(s, n_groups - 1) < n_ctx_groups

    def load(rb):
        if not first:
            return x_refs[0][rb * CHUNK:(rb + 1) * CHUNK, :]
        return jnp.where(is_ctx, x_refs[1][rb], x_refs[0][rb])

    def prepare(h_ref):
        for rb in range(nb):
            m = jnp.where(is_ctx, nb, rb)
            gain = g_ref[...] * (1.0 + mod_ref[m, 1:2, :])
            h = _rms(load(rb)) * gain + mod_ref[m, 0:1, :]
            h_ref[rb * CHUNK:(rb + 1) * CHUNK, :] = h.astype(BF16)

    def project(h_new, h_old):
        prepare(h_new)
        h = h_old[...]
        o_ref[:, 0:COL_D] = jnp.dot(h, w_ref[...], preferred_element_type=F32).astype(BF16)
        lane = lax.broadcasted_iota(jnp.int32, wd_ref.shape, 1)
        wd = jnp.where(lane < P_COLS - COL_D, wd_ref[...], jnp.zeros_like(wd_ref))
        o_ref[:, COL_D:P_PAD] = jnp.dot(h, wd, preferred_element_type=F32).astype(BF16)

    @pl.when(s == 0)
    def _():
        prepare(h0_ref)

    @pl.when((s > 0) & (s % 2 == 1))
    def _():
        project(h1_ref, h0_ref)

    @pl.when((s > 0) & (s % 2 == 0))
    def _():
        project(h0_ref, h1_ref)


def _inproj_call(x_args, mod, g_pre, w_in, layer, first, nb, n_groups, n_ctx_groups):
    d = D_MODEL
    rows = nb * CHUNK
    prep = lambda s: jnp.minimum(s, n_groups - 1)
    post = lambda s: jnp.maximum(s - 1, 0)
    kern = functools.partial(_inproj_kernel, first=first, nb=nb, n_groups=n_groups,
                             n_ctx_groups=n_ctx_groups)
    return pl.pallas_call(
        kern,
        grid=(n_groups + 1,),
        in_specs=_x_specs(first, nb, 1, n_ctx_groups, d, prep) + [
            pl.BlockSpec((None, MOD_ROWS, 6, d), lambda s: (layer, 0, 0, 0)),
            pl.BlockSpec((None, 1, d), lambda s: (layer, 0, 0)),
            pl.BlockSpec((None, d, COL_D), lambda s: (layer, 0, 0),
                         pipeline_mode=pl.Buffered(1)),
            pl.BlockSpec((None, d, D_BLOCK), lambda s: (layer, 0, COL_D // D_BLOCK),
                         pipeline_mode=pl.Buffered(1)),
        ],
        out_specs=pl.BlockSpec((rows, P_PAD), lambda s: (post(s), 0)),
        out_shape=jax.ShapeDtypeStruct((n_groups * rows, P_PAD), BF16),
        scratch_shapes=[pltpu.VMEM((rows, d), BF16), pltpu.VMEM((rows, d), BF16)],
        compiler_params=_params("arbitrary"),
        name="inproj",
    )(*x_args, mod, g_pre, w_in, w_in)


LOG2_E = 1.4426950408889634


def _log2_sigmoid(z):
    return jnp.minimum(z, 0.0) * LOG2_E - jnp.log2(1.0 + jnp.exp2(jnp.abs(z) * -LOG2_E))


def _cumsum_rows(tri, la):
    hi = la.astype(BF16)
    lo = (la - hi.astype(F32)).astype(BF16)
    return jnp.dot(jnp.concatenate([tri, tri], axis=1), jnp.concatenate([hi, lo], axis=0),
                   preferred_element_type=F32)


def _gla_direction(q_ref, k_ref, v_ref, d_ref, wd_ref, bd_ref, s_ref, o_ref, reverse, nb):
    row = lax.broadcasted_iota(jnp.int32, (CHUNK, CHUNK), 0)
    col = lax.broadcasted_iota(jnp.int32, (CHUNK, CHUNK), 1)
    causal = (col >= row) if reverse else (col <= row)
    eye = col == row
    tri = jnp.where(causal, 1.0, 0.0).astype(BF16)
    end = 0 if reverse else CHUNK - 1
    mid = CHUNK // 2

    z = jnp.dot(d_ref[...], wd_ref[...], preferred_element_type=F32) + bd_ref[...]
    la_all = _log2_sigmoid(z) * (1.0 / GATE_TAU)

    for b in range(nb):
        rows = slice(b * CHUNK, (b + 1) * CHUNK)
        cum = _cumsum_rows(tri, la_all[rows])
        c_end = cum[end:end + 1, :]
        c_mid = cum[mid:mid + 1, :]
        e_mid = jnp.exp2(c_mid)
        e_end_mid = jnp.exp2(c_end - c_mid)
        e_end = jnp.exp2(c_end)
        c_mid_q = c_mid - LOG2_Q_SCALE
        for h in range(B_HEADS):
            kc = slice(h * B_DK, (h + 1) * B_DK)
            vc = slice(h * B_DV, (h + 1) * B_DV)
            q = q_ref[rows, kc].astype(F32)
            k = k_ref[rows, kc].astype(F32)
            v = v_ref[rows, vc]
            q_t = q * jnp.exp2(cum[:, kc] - c_mid_q[:, kc])
            k_t = k * jnp.exp2(c_mid[:, kc] - cum[:, kc])
            q_in = (q_t * e_mid[:, kc]).astype(BF16)
            k_d = k_t * e_end_mid[:, kc]
            scores = lax.dot_general(q_t.astype(BF16), k_t.astype(BF16),
                                     (((1,), (1,)), ((), ())), preferred_element_type=F32)
            scores = jnp.where(causal, scores, 0.0).astype(BF16)
            s = s_ref[b, h]
            o = jnp.dot(scores, v, preferred_element_type=F32)
            o = o + jnp.dot(q_in, s.astype(BF16), preferred_element_type=F32)
            o_ref[rows, vc] = o
            decay = jnp.broadcast_to(e_end[:, kc], (B_DK, B_DK))
            decay_col = jnp.sum(jnp.where(eye, decay, 0.0), axis=-1, keepdims=True)
            s_ref[b, h] = decay_col * s + jnp.dot(k_d.T.astype(BF16), v,
                                                  preferred_element_type=F32)


def _gla_kernel(qf_ref, kf_ref, vf_ref, df_ref, qb_ref, kb_ref, vb_ref, db_ref,
                wdf_ref, bdf_ref, wdb_ref, bdb_ref, of_ref, ob_ref, s_ref, *, nb):
    @pl.when(pl.program_id(0) == 0)
    def _():
        s_ref[...] = jnp.zeros_like(s_ref)

    _gla_direction(qf_ref, kf_ref, vf_ref, df_ref, wdf_ref, bdf_ref, s_ref.at[0], of_ref,
                   False, nb)
    _gla_direction(qb_ref, kb_ref, vb_ref, db_ref, wdb_ref, bdb_ref, s_ref.at[1], ob_ref,
                   True, nb)


def _gla_call(p, wd_f, bd_f, wd_b, bd_b, layer, nb, n_groups, n_ctx_groups):
    rows = nb * CHUNK

    def fwd(s):
        return s

    def bwd(s):
        return jnp.where(s < n_ctx_groups, n_ctx_groups - 1 - s, n_groups + n_ctx_groups - 1 - s)

    def chunk_specs(order):
        return [
            pl.BlockSpec((rows, B_KEY_W), lambda s: (order(s), COL_Q // B_KEY_W)),
            pl.BlockSpec((rows, B_KEY_W), lambda s: (order(s), COL_K // B_KEY_W)),
            pl.BlockSpec((rows, B_WIDTH), lambda s: (order(s), COL_VV // B_WIDTH)),
            pl.BlockSpec((rows, D_BLOCK), lambda s: (order(s), COL_D // D_BLOCK)),
        ]

    def weight_specs():
        return [
            pl.BlockSpec((None, D_BLOCK, B_KEY_W), lambda s: (layer, 0, 0)),
            pl.BlockSpec((None, 1, B_KEY_W), lambda s: (layer, 0, 0)),
        ]

    out = jax.ShapeDtypeStruct((n_groups * rows, B_WIDTH), F32)
    return pl.pallas_call(
        functools.partial(_gla_kernel, nb=nb),
        grid=(n_groups,),
        in_specs=chunk_specs(fwd) + chunk_specs(bwd) + weight_specs() + weight_specs(),
        out_specs=[
            pl.BlockSpec((rows, B_WIDTH), lambda s: (fwd(s), 0)),
            pl.BlockSpec((rows, B_WIDTH), lambda s: (bwd(s), 0)),
        ],
        out_shape=[out, out],
        scratch_shapes=[pltpu.VMEM((2, nb, B_HEADS, B_DK, B_DV), F32)],
        compiler_params=_params("arbitrary"),
        name="gla",
    )(p, p, p, p, p, p, p, p, wd_f, bd_f, wd_b, bd_b)


def _mix_prepare(u_ref, v_ref, of_ref, ob_ref, g_ref, lng_ref, lnb_ref, ws_ref, bs_ref,
                 og_ref, cat_ref, nb):
    for rb in range(nb):
        rows = slice(rb * CHUNK, (rb + 1) * CHUNK)
        for h in range(A_HEADS):
            cols = slice(h * A_HEAD_DIM, (h + 1) * A_HEAD_DIM)
            v = _gelu_tanh(v_ref[rows, cols].astype(F32))
            mu = jnp.mean(v, axis=-1, keepdims=True)
            vc = v - mu
            var = jnp.mean(vc * vc, axis=-1, keepdims=True)
            vn = vc * lax.rsqrt(var + EPS) * lng_ref[:, cols] + lnb_ref[:, cols]
            mixed = jnp.dot(ws_ref[h], vn.astype(BF16), preferred_element_type=F32)
            mixed = mixed + bs_ref[:, h:h + 1]
            u = _gelu_tanh(u_ref[rows, cols].astype(F32))
            cat_ref[rows, cols] = (u * mixed).astype(BF16)
        for h in range(B_HEADS):
            cols = slice(h * B_DV, (h + 1) * B_DV)
            o = of_ref[rows, cols] + ob_ref[rows, cols]
            y = _rms(o) * og_ref[:, cols]
            gated = y * _silu(g_ref[rows, cols].astype(F32))
            cat_ref[rows, A_WIDTH + h * B_DV:A_WIDTH + (h + 1) * B_DV] = gated.astype(BF16)


def _mix_kernel(*refs, first, nb, group0, n_ctx_groups):
    n_x = 2 if first else 1
    prep_refs = refs[:5]
    x_refs = refs[5:5 + n_x]
    (lng_ref, lnb_ref, ws_ref, bs_ref, mod_ref, og_ref, gpost_ref, wo_ref,
     o_ref, cat0_ref, cat1_ref, y_ref) = refs[5 + n_x:]
    s = pl.program_id(0)

    def prepare(cat_ref):
        _mix_prepare(*prep_refs, lng_ref, lnb_ref, ws_ref, bs_ref, og_ref, cat_ref, nb)

    def project(cat_new, cat_old):
        prepare(cat_new)
        y_ref[...] = jnp.dot(cat_old[...], wo_ref[...], preferred_element_type=F32)

    @pl.when(s == 0)
    def _():
        prepare(cat0_ref)

    @pl.when((s > 0) & (s % 2 == 1))
    def _():
        project(cat1_ref, cat0_ref)

    @pl.when((s > 0) & (s % 2 == 0))
    def _():
        project(cat0_ref, cat1_ref)

    @pl.when(s > 0)
    def _():
        group = s - 1 + group0

        def epilogue(load):
            def residual(rb):
                m = _mod_row(group, rb, nb, n_ctx_groups)
                gain = gpost_ref[...] * mod_ref[m, 2:3, :]
                rows = _chunk_rows(rb)

                def write(cols, val):
                    o_ref[rows, cols] = val
                _norm_residual(lambda cols: y_ref[rows, cols], lambda cols: load(rb, cols),
                               write, gain)
            _for_chunks(nb, residual)
        _with_source(first, group, n_ctx_groups, nb, x_refs, epilogue)


def _mix_call(p, o_f, o_b, x_args, ln_g, ln_b, w_s, b_s, mod, out_g, g_post, w_out, layer,
              first, nb, group0, n_groups, n_ctx_groups):
    d = D_MODEL
    rows = nb * CHUNK
    n_tiles = n_groups - group0
    prep = lambda s: group0 + jnp.minimum(s, n_tiles - 1)
    post = lambda s: group0 + jnp.maximum(s - 1, 0)
    kern = functools.partial(_mix_kernel, first=first, nb=nb, group0=group0,
                             n_ctx_groups=n_ctx_groups)
    return pl.pallas_call(
        kern,
        grid=(n_tiles + 1,),
        in_specs=[
            pl.BlockSpec((rows, A_WIDTH), lambda s: (prep(s), COL_U // A_WIDTH)),
            pl.BlockSpec((rows, A_WIDTH), lambda s: (prep(s), COL_V // A_WIDTH)),
            pl.BlockSpec((rows, B_WIDTH), lambda s: (prep(s), 0)),
            pl.BlockSpec((rows, B_WIDTH), lambda s: (prep(s), 0)),
            pl.BlockSpec((rows, B_WIDTH), lambda s: (prep(s), COL_G // B_WIDTH)),
        ] + _x_specs(first, nb, 1, n_ctx_groups, d, post) + [
            pl.BlockSpec((None, 1, A_WIDTH), lambda s: (layer, 0, 0)),
            pl.BlockSpec((None, 1, A_WIDTH), lambda s: (layer, 0, 0)),
            pl.BlockSpec((None, A_HEADS, CHUNK, CHUNK), lambda s: (layer, 0, 0, 0)),
            pl.BlockSpec((None, CHUNK, A_HEADS), lambda s: (layer, 0, 0)),
            pl.BlockSpec((None, MOD_ROWS, 6, d), lambda s: (layer, 0, 0, 0)),
            pl.BlockSpec((None, 1, B_WIDTH), lambda s: (layer, 0, 0)),
            pl.BlockSpec((None, 1, d), lambda s: (layer, 0, 0)),
            pl.BlockSpec((None, A_WIDTH + B_WIDTH, d), lambda s: (layer, 0, 0),
                         pipeline_mode=pl.Buffered(1)),
        ],
        out_specs=pl.BlockSpec((rows, d), lambda s: (post(s), 0)),
        out_shape=jax.ShapeDtypeStruct((n_groups * rows, d), F32),
        scratch_shapes=[pltpu.VMEM((rows, A_WIDTH + B_WIDTH), BF16),
                        pltpu.VMEM((rows, A_WIDTH + B_WIDTH), BF16),
                        pltpu.VMEM((rows, d), F32)],
        input_output_aliases={} if first else {5: 0},
        compiler_params=_params("arbitrary"),
        name="mix",
    )(p, p, o_f, o_b, p, *x_args, ln_g, ln_b, w_s, b_s, mod, out_g, g_post, w_out)


def _ffn_kernel(x_ref, mod_ref, gpre_ref, gpost_ref, wa_ref, wb_ref, wo_ref, o_ref,
                h_ref, act_ref, *, last, nb, group0, n_ctx_groups):
    j = pl.program_id(1)
    group = pl.program_id(0) + group0

    @pl.when(j == 0)
    def _():
        def body(rb):
            m = _mod_row(group, rb, nb, n_ctx_groups)
            gain = gpre_ref[...] * (1.0 + mod_ref[m, 4:5, :])
            h = _rms(x_ref[_chunk_rows(rb), :]) * gain + mod_ref[m, 3:4, :]
            h_ref[_chunk_rows(rb), :] = h.astype(BF16)
        _for_chunks(nb, body)

    @pl.when(j < FFN_NH)
    def _():
        h = h_ref[...]
        a = jnp.dot(h, wa_ref[...], preferred_element_type=F32)
        b = jnp.dot(h, wb_ref[...], preferred_element_type=F32)
        act_ref[j] = (_silu(a) * b).astype(BF16)

    @pl.when(j == FFN_NH)
    def _():
        for n in range(FFN_NO):
            cols = slice(n * FFN_TN, (n + 1) * FFN_TN)
            acc = jnp.dot(act_ref[0], wo_ref[0:FFN_TH, cols], preferred_element_type=F32)
            for k in range(1, FFN_NH):
                acc += jnp.dot(act_ref[k], wo_ref[k * FFN_TH:(k + 1) * FFN_TH, cols],
                               preferred_element_type=F32)
            if last:
                for b in range(nb):
                    o_ref[b, :, cols] = acc[b * CHUNK:(b + 1) * CHUNK, :]
            else:
                o_ref[:, cols] = acc

        def body(rb):
            rows = _chunk_rows(rb)
            m = _mod_row(group, rb, nb, n_ctx_groups)
            gain = gpost_ref[...] * mod_ref[m, 5:6, :]

            def read_y(cols):
                return o_ref[rb, :, cols] if last else o_ref[rows, cols]

            def write(cols, val):
                if last:
                    o_ref[rb, :, cols] = val
                else:
                    o_ref[rows, cols] = val
            _norm_residual(read_y, lambda cols: x_ref[rows, cols], write, gain)
        _for_chunks(nb, body)


def _ffn_call(xs, mod, g_pre, g_post, w_ffn_in, w_ffn_out, layer, last, nb, group0,
              n_groups, n_ctx_groups):
    d = D_MODEL
    rows = nb * CHUNK
    last_h = FFN_NH - 1
    tile = lambda i: i + group0
    if last:
        out_spec = pl.BlockSpec((nb, CHUNK, d), lambda i, j: (0, i, 0))
        out_shape = jax.ShapeDtypeStruct((nb, (n_groups - group0) * CHUNK, d), F32)
    else:
        out_spec = pl.BlockSpec((rows, d), lambda i, j: (tile(i), 0))
        out_shape = jax.ShapeDtypeStruct((n_groups * rows, d), F32)
    kern = functools.partial(_ffn_kernel, last=last, nb=nb, group0=group0,
                             n_ctx_groups=n_ctx_groups)
    return pl.pallas_call(
        kern,
        grid=(n_groups - group0, FFN_NH + 1),
        in_specs=[
            pl.BlockSpec((rows, d), lambda i, j: (tile(i), 0)),
            pl.BlockSpec((None, MOD_ROWS, 6, d), lambda i, j: (layer, 0, 0, 0)),
            pl.BlockSpec((None, 1, d), lambda i, j: (layer, 0, 0)),
            pl.BlockSpec((None, 1, d), lambda i, j: (layer, 0, 0)),
            pl.BlockSpec((None, d, FFN_TH), lambda i, j: (layer, 0, jnp.minimum(j, last_h))),
            pl.BlockSpec((None, d, FFN_TH),
                         lambda i, j: (layer, 0, FFN_NH + jnp.minimum(j, last_h))),
            pl.BlockSpec((None, FFN_HIDDEN, d), lambda i, j: (layer, 0, 0),
                         pipeline_mode=pl.Buffered(1)),
        ],
        out_specs=out_spec,
        out_shape=out_shape,
        scratch_shapes=[pltpu.VMEM((rows, d), BF16),
                        pltpu.VMEM((FFN_NH, rows, FFN_TH), BF16)],
        input_output_aliases={} if last else {0: 0},
        compiler_params=_params("arbitrary", "arbitrary", vmem_limit=FFN_VMEM_LIMIT),
        name="ffn",
    )(xs, mod, g_pre, g_post, w_ffn_in, w_ffn_in, w_ffn_out)


def kernel(x, c, ctx, c_ctx, w_mod, b_mod, g_pre_mix, g_post_mix, g_pre_ffn, g_post_ffn, w_in, gmlp_ln_g, gmlp_ln_b, gmlp_ws, gmlp_bs, gla_wd2_fwd, gla_bd_fwd, gla_wd2_bwd, gla_bd_bwd, gla_out_g, w_out, w_ffn_in, w_ffn_out):
    nb, seq, d = x.shape
    ctx_len = ctx.shape[1]
    n_layers = w_mod.shape[0]
    assert d == D_MODEL and nb + 1 <= MOD_ROWS
    assert seq % CHUNK == 0 and ctx_len % CHUNK == 0
    n_ctx_groups = ctx_len // CHUNK
    n_groups = n_ctx_groups + seq // CHUNK

    act = jnp.zeros((MOD_ROWS, d), F32).at[:nb].set(c).at[nb].set(c_ctx)
    mod = _mod_call(act, w_mod, b_mod).reshape(n_layers, MOD_ROWS, 6, d)

    row = lambda t: t.reshape(n_layers, 1, t.shape[-1])
    w_in_b = w_in.astype(BF16)
    w_out_b = w_out.astype(BF16)
    w_ffn_in_b = w_ffn_in.astype(BF16)
    w_ffn_out_b = w_ffn_out.astype(BF16)
    w_s_b = gmlp_ws.astype(BF16)
    b_s_t = jnp.swapaxes(gmlp_bs, 1, 2)
    wd_f = jnp.pad(gla_wd2_fwd, ((0, 0), (0, D_BLOCK - GATE_RANK), (0, 0))).astype(BF16)
    wd_b = jnp.pad(gla_wd2_bwd, ((0, 0), (GATE_RANK, D_BLOCK - 2 * GATE_RANK), (0, 0))).astype(BF16)

    x_args = (x, ctx)
    for l in range(n_layers):
        first = l == 0
        last = l == n_layers - 1
        group0 = n_ctx_groups if last else 0
        p = _inproj_call(x_args, mod, row(g_pre_mix), w_in_b, l, first, nb, n_groups,
                         n_ctx_groups)
        o_f, o_b = _gla_call(p, wd_f, row(gla_bd_fwd), wd_b, row(gla_bd_bwd), l, nb,
                             n_groups, n_ctx_groups)
        xs = _mix_call(p, o_f, o_b, x_args, row(gmlp_ln_g), row(gmlp_ln_b), w_s_b, b_s_t, mod,
                       row(gla_out_g), row(g_post_mix), w_out_b, l, first, nb, group0,
                       n_groups, n_ctx_groups)
        xs = _ffn_call(xs, mod, row(g_pre_ffn), row(g_post_ffn), w_ffn_in_b, w_ffn_out_b,
                       l, last, nb, group0, n_groups, n_ctx_groups)
        x_args = (xs,)
    return xs
```

```python
import functools

import jax
import jax.numpy as jnp
from jax import lax
from jax.experimental import pallas as pl
from jax.experimental.pallas import tpu as pltpu

F32 = jnp.float32
BF16 = jnp.bfloat16

D_MODEL = 2048
CHUNK = 128
A_WIDTH = 1024
A_HEADS = 8
A_HEAD_DIM = A_WIDTH // A_HEADS
B_WIDTH = 1024
B_HEADS = 4
B_DV = B_WIDTH // B_HEADS
B_DK = B_DV // 2
B_KEY_W = B_HEADS * B_DK
GATE_RANK = 16
GATE_TAU = 16.0
LOG2_Q_SCALE = -3.5
FFN_HIDDEN = 5632
EPS = 1e-6

COL_U = 0
COL_V = A_WIDTH
COL_Q = 2 * A_WIDTH
COL_K = COL_Q + B_KEY_W
COL_VV = COL_K + B_KEY_W
COL_G = COL_VV + B_WIDTH
COL_D = COL_G + B_WIDTH
P_COLS = COL_D + 2 * GATE_RANK
D_BLOCK = 128
P_PAD = COL_D + D_BLOCK

FFN_TH = 512
FFN_TN = 512
FFN_NH = FFN_HIDDEN // FFN_TH
FFN_NO = D_MODEL // FFN_TN
MOD_TN = 1024
MOD_ROWS = 8
NORM_COLS = 512

VMEM_BYTES = 64 * 1024 * 1024
VMEM_LIMIT = VMEM_BYTES - 8 * 1024 * 1024
HIGH_VMEM_LIMIT = VMEM_BYTES - 4 * 1024 * 1024


def _params(*sem, vmem_limit=VMEM_LIMIT):
    return pltpu.CompilerParams(dimension_semantics=sem, vmem_limit_bytes=vmem_limit)


def _silu(x):
    return x * jax.nn.sigmoid(x)


def _gelu_tanh(x):
    k = 0.7978845608028654
    return x * (0.5 + 0.5 * jnp.tanh(x * (k + (k * 0.044715) * (x * x))))


def _rms(x):
    return x * lax.rsqrt(jnp.mean(x * x, axis=-1, keepdims=True) + EPS)


def _chunk_rows(rb):
    return pl.ds(pl.multiple_of(rb * CHUNK, CHUNK), CHUNK)


def _for_chunks(n, body):
    def step(rb, carry):
        body(rb)
        return carry
    lax.fori_loop(0, n, step, 0)


def _mod_row(group, rb, nb, n_ctx_groups):
    g = group + rb // nb
    return jnp.where(g < n_ctx_groups, nb, rb % nb)


def _load_canonical(ref):
    return lambda rb, cols=slice(None): ref[_chunk_rows(rb), cols]


def _load_sample_major(ref, nb):
    return lambda rb, cols=slice(None): ref[rb % nb, _chunk_rows(rb // nb), cols]


def _norm_residual(read_y, read_x, write, gain):
    blocks = [slice(c, c + NORM_COLS) for c in range(0, D_MODEL, NORM_COLS)]
    ss = None
    for cols in blocks:
        y = read_y(cols)
        part = jnp.sum(y * y, axis=-1, keepdims=True)
        ss = part if ss is None else ss + part
    inv = lax.rsqrt(ss * (1.0 / D_MODEL) + EPS)
    for cols in blocks:
        write(cols, read_x(cols) + read_y(cols) * inv * gain[:, cols])


def _mod_kernel(act_ref, w_ref, b_ref, o_ref):
    a = _silu(act_ref[...])
    o_ref[...] = jnp.dot(a.astype(BF16), w_ref[...].astype(BF16),
                         preferred_element_type=F32) + b_ref[...]


def _mod_call(act, w_mod, b_mod):
    n_layers, d, n = w_mod.shape
    return pl.pallas_call(
        _mod_kernel,
        grid=(n_layers, n // MOD_TN),
        in_specs=[
            pl.BlockSpec((MOD_ROWS, d), lambda l, j: (0, 0)),
            pl.BlockSpec((None, d, MOD_TN), lambda l, j: (l, 0, j)),
            pl.BlockSpec((None, 1, MOD_TN), lambda l, j: (l, 0, j)),
        ],
        out_specs=pl.BlockSpec((None, MOD_ROWS, MOD_TN), lambda l, j: (l, 0, j)),
        out_shape=jax.ShapeDtypeStruct((n_layers, MOD_ROWS, n), F32),
        compiler_params=_params("arbitrary", "arbitrary"),
        name="mod",
    )(act, w_mod, b_mod.reshape(n_layers, 1, n))


def _x_specs(first, nb, groups, n_ctx_tiles, d, tile_of):
    if not first:
        return [pl.BlockSpec((groups * nb * CHUNK, d), lambda *ix: (tile_of(*ix), 0))]
    blk = (nb, groups * CHUNK, d)
    return [
        pl.BlockSpec(blk, lambda *ix: (0, jnp.maximum(tile_of(*ix) - n_ctx_tiles, 0), 0)),
        pl.BlockSpec(blk, lambda *ix: (0, jnp.minimum(tile_of(*ix), n_ctx_tiles - 1), 0),
                     pipeline_mode=pl.Buffered(1)),
    ]


def _with_source(first, tile, n_ctx_tiles, nb, x_refs, fn):
    if not first:
        fn(_load_canonical(x_refs[0]))
        return

    @pl.when(tile >= n_ctx_tiles)
    def _():
        fn(_load_sample_major(x_refs[0], nb))

    @pl.when(tile < n_ctx_tiles)
    def _():
        fn(_load_sample_major(x_refs[1], nb))


def _inproj_kernel(*refs, first, nb, n_groups, n_ctx_groups):
    n_x = 2 if first else 1
    x_refs = refs[:n_x]
    mod_ref, g_ref, w_ref, wd_ref, o_ref, h0_ref, h1_ref = refs[n_x:]
    s = pl.program_id(0)
    is_ctx = jnp.minimum(s, n_groups - 1) < n_ctx_groups

    def load(rb):
        if not first:
            return x_refs[0][rb * CHUNK:(rb + 1) * CHUNK, :]
        return jnp.where(is_ctx, x_refs[1][rb], x_refs[0][rb])

    def prepare(h_ref):
        for rb in range(nb):
            m = jnp.where(is_ctx, nb, rb)
            gain = g_ref[...] * (1.0 + mod_ref[m, 1:2, :])
            h = _rms(load(rb)) * gain + mod_ref[m, 0:1, :]
            h_ref[rb * CHUNK:(rb + 1) * CHUNK, :] = h.astype(BF16)

    def project(h_new, h_old):
        h = h_old[...]
        o_ref[:, 0:COL_D] = jnp.dot(h, w_ref[...], preferred_element_type=F32).astype(BF16)
        lane = lax.broadcasted_iota(jnp.int32, wd_ref.shape, 1)
        wd = jnp.where(lane < P_COLS - COL_D, wd_ref[...], jnp.zeros_like(wd_ref))
        o_ref[:, COL_D:P_PAD] = jnp.dot(h, wd, preferred_element_type=F32).astype(BF16)
        prepare(h_new)

    @pl.when(s == 0)
    def _():
        prepare(h0_ref)

    @pl.when((s > 0) & (s % 2 == 1))
    def _():
        project(h1_ref, h0_ref)

    @pl.when((s > 0) & (s % 2 == 0))
    def _():
        project(h0_ref, h1_ref)


def _inproj_call(x_args, mod, g_pre, w_in, layer, first, nb, n_groups, n_ctx_groups):
    d = D_MODEL
    rows = nb * CHUNK
    prep = lambda s: jnp.minimum(s, n_groups - 1)
    post = lambda s: jnp.maximum(s - 1, 0)
    kern = functools.partial(_inproj_kernel, first=first, nb=nb, n_groups=n_groups,
                             n_ctx_groups=n_ctx_groups)
    return pl.pallas_call(
        kern,
        grid=(n_groups + 1,),
        in_specs=_x_specs(first, nb, 1, n_ctx_groups, d, prep) + [
            pl.BlockSpec((None, MOD_ROWS, 6, d), lambda s: (layer, 0, 0, 0)),
            pl.BlockSpec((None, 1, d), lambda s: (layer, 0, 0)),
            pl.BlockSpec((None, d, COL_D), lambda s: (layer, 0, 0),
                         pipeline_mode=pl.Buffered(1)),
            pl.BlockSpec((None, d, D_BLOCK), lambda s: (layer, 0, COL_D // D_BLOCK),
                         pipeline_mode=pl.Buffered(1)),
        ],
        out_specs=pl.BlockSpec((rows, P_PAD), lambda s: (post(s), 0)),
        out_shape=jax.ShapeDtypeStruct((n_groups * rows, P_PAD), BF16),
        scratch_shapes=[pltpu.VMEM((rows, d), BF16), pltpu.VMEM((rows, d), BF16)],
        compiler_params=_params("arbitrary"),
        name="inproj",
    )(*x_args, mod, g_pre, w_in, w_in)


LOG2_E = 1.4426950408889634


def _log2_sigmoid(z):
    return jnp.minimum(z, 0.0) * LOG2_E - jnp.log2(1.0 + jnp.exp2(jnp.abs(z) * -LOG2_E))


def _cumsum_rows(tri, la):
    hi = la.astype(BF16)
    lo = (la - hi.astype(F32)).astype(BF16)
    return jnp.dot(jnp.concatenate([tri, tri], axis=1), jnp.concatenate([hi, lo], axis=0),
                   preferred_element_type=F32)


def _gla_direction(q_ref, k_ref, v_ref, d_ref, wd_ref, bd_ref, s_ref, o_ref, reverse, nb):
    row = lax.broadcasted_iota(jnp.int32, (CHUNK, CHUNK), 0)
    col = lax.broadcasted_iota(jnp.int32, (CHUNK, CHUNK), 1)
    causal = (col >= row) if reverse else (col <= row)
    eye = col == row
    tri = jnp.where(causal, 1.0, 0.0).astype(BF16)
    end = 0 if reverse else CHUNK - 1
    mid = CHUNK // 2

    z = jnp.dot(d_ref[...], wd_ref[...], preferred_element_type=F32) + bd_ref[...]
    la_all = _log2_sigmoid(z) * (1.0 / GATE_TAU)

    for b in range(nb):
        rows = slice(b * CHUNK, (b + 1) * CHUNK)
        cum = _cumsum_rows(tri, la_all[rows])
        c_end = cum[end:end + 1, :]
        c_mid = cum[mid:mid + 1, :]
        e_mid = jnp.exp2(c_mid)
        e_end_mid = jnp.exp2(c_end - c_mid)
        e_end = jnp.exp2(c_end)
        c_mid_q = c_mid - LOG2_Q_SCALE
        for h in range(B_HEADS):
            kc = slice(h * B_DK, (h + 1) * B_DK)
            vc = slice(h * B_DV, (h + 1) * B_DV)
            q = q_ref[rows, kc].astype(F32)
            k = k_ref[rows, kc].astype(F32)
            v = v_ref[rows, vc]
            q_t = q * jnp.exp2(cum[:, kc] - c_mid_q[:, kc])
            k_t = k * jnp.exp2(c_mid[:, kc] - cum[:, kc])
            q_in = (q_t * e_mid[:, kc]).astype(BF16)
            k_d = k_t * e_end_mid[:, kc]
            scores = lax.dot_general(q_t.astype(BF16), k_t.astype(BF16),
                                     (((1,), (1,)), ((), ())), preferred_element_type=F32)
            scores = jnp.where(causal, scores, 0.0).astype(BF16)
            s = s_ref[b, h]
            o = jnp.dot(scores, v, preferred_element_type=F32)
            o = o + jnp.dot(q_in, s.astype(BF16), preferred_element_type=F32)
            o_ref[rows, vc] = o
            decay = jnp.broadcast_to(e_end[:, kc], (B_DK, B_DK))
            decay_col = jnp.sum(jnp.where(eye, decay, 0.0), axis=-1, keepdims=True)
            s_ref[b, h] = decay_col * s + jnp.dot(k_d.T.astype(BF16), v,
                                                  preferred_element_type=F32)


def _gla_kernel(qf_ref, kf_ref, vf_ref, df_ref, qb_ref, kb_ref, vb_ref, db_ref,
                wdf_ref, bdf_ref, wdb_ref, bdb_ref, of_ref, ob_ref, s_ref, *, nb):
    @pl.when(pl.program_id(0) == 0)
    def _():
        s_ref[...] = jnp.zeros_like(s_ref)

    _gla_direction(qf_ref, kf_ref, vf_ref, df_ref, wdf_ref, bdf_ref, s_ref.at[0], of_ref,
                   False, nb)
    _gla_direction(qb_ref, kb_ref, vb_ref, db_ref, wdb_ref, bdb_ref, s_ref.at[1], ob_ref,
                   True, nb)


def _gla_call(p, wd_f, bd_f, wd_b, bd_b, layer, nb, n_groups, n_ctx_groups):
    rows = nb * CHUNK

    def fwd(s):
        return s

    def bwd(s):
        return jnp.where(s < n_ctx_groups, n_ctx_groups - 1 - s, n_groups + n_ctx_groups - 1 - s)

    def chunk_specs(order):
        return [
            pl.BlockSpec((rows, B_KEY_W), lambda s: (order(s), COL_Q // B_KEY_W)),
            pl.BlockSpec((rows, B_KEY_W), lambda s: (order(s), COL_K // B_KEY_W)),
            pl.BlockSpec((rows, B_WIDTH), lambda s: (order(s), COL_VV // B_WIDTH)),
            pl.BlockSpec((rows, D_BLOCK), lambda s: (order(s), COL_D // D_BLOCK)),
        ]

    def weight_specs():
        return [
            pl.BlockSpec((None, D_BLOCK, B_KEY_W), lambda s: (layer, 0, 0)),
            pl.BlockSpec((None, 1, B_KEY_W), lambda s: (layer, 0, 0)),
        ]

    out = jax.ShapeDtypeStruct((n_groups * rows, B_WIDTH), F32)
    return pl.pallas_call(
        functools.partial(_gla_kernel, nb=nb),
        grid=(n_groups,),
        in_specs=chunk_specs(fwd) + chunk_specs(bwd) + weight_specs() + weight_specs(),
        out_specs=[
            pl.BlockSpec((rows, B_WIDTH), lambda s: (fwd(s), 0)),
            pl.BlockSpec((rows, B_WIDTH), lambda s: (bwd(s), 0)),
        ],
        out_shape=[out, out],
        scratch_shapes=[pltpu.VMEM((2, nb, B_HEADS, B_DK, B_DV), F32)],
        compiler_params=_params("arbitrary"),
        name="gla",
    )(p, p, p, p, p, p, p, p, wd_f, bd_f, wd_b, bd_b)


def _mix_prepare(u_ref, v_ref, of_ref, ob_ref, g_ref, lng_ref, lnb_ref, ws_ref, bs_ref,
                 og_ref, cat_ref, nb):
    for rb in range(nb):
        rows = slice(rb * CHUNK, (rb + 1) * CHUNK)
        for h in range(A_HEADS):
            cols = slice(h * A_HEAD_DIM, (h + 1) * A_HEAD_DIM)
            v = _gelu_tanh(v_ref[rows, cols].astype(F32))
            mu = jnp.mean(v, axis=-1, keepdims=True)
            vc = v - mu
            var = jnp.mean(vc * vc, axis=-1, keepdims=True)
            vn = vc * lax.rsqrt(var + EPS) * lng_ref[:, cols] + lnb_ref[:, cols]
            mixed = jnp.dot(ws_ref[h], vn.astype(BF16), preferred_element_type=F32)
            mixed = mixed + bs_ref[:, h:h + 1]
            u = _gelu_tanh(u_ref[rows, cols].astype(F32))
            cat_ref[rows, cols] = (u * mixed).astype(BF16)
        for h in range(B_HEADS):
            cols = slice(h * B_DV, (h + 1) * B_DV)
            o = of_ref[rows, cols] + ob_ref[rows, cols]
            y = _rms(o) * og_ref[:, cols]
            gated = y * _silu(g_ref[rows, cols].astype(F32))
            cat_ref[rows, A_WIDTH + h * B_DV:A_WIDTH + (h + 1) * B_DV] = gated.astype(BF16)


def _mix_kernel(*refs, first, nb, group0, n_ctx_groups):
    n_x = 2 if first else 1
    prep_refs = refs[:5]
    x_refs = refs[5:5 + n_x]
    (lng_ref, lnb_ref, ws_ref, bs_ref, mod_ref, og_ref, gpost_ref, wo_ref,
     o_ref, cat0_ref, cat1_ref, y_ref) = refs[5 + n_x:]
    s = pl.program_id(0)

    def prepare(cat_ref):
        _mix_prepare(*prep_refs, lng_ref, lnb_ref, ws_ref, bs_ref, og_ref, cat_ref, nb)

    def project(cat_new, cat_old):
        y_ref[...] = jnp.dot(cat_old[...], wo_ref[...], preferred_element_type=F32)
        prepare(cat_new)

    @pl.when(s == 0)
    def _():
        prepare(cat0_ref)

    @pl.when((s > 0) & (s % 2 == 1))
    def _():
        project(cat1_ref, cat0_ref)

    @pl.when((s > 0) & (s % 2 == 0))
    def _():
        project(cat0_ref, cat1_ref)

    @pl.when(s > 0)
    def _():
        group = s - 1 + group0

        def epilogue(load):
            def residual(rb):
                m = _mod_row(group, rb, nb, n_ctx_groups)
                gain = gpost_ref[...] * mod_ref[m, 2:3, :]
                rows = _chunk_rows(rb)

                def write(cols, val):
                    o_ref[rows, cols] = val
                _norm_residual(lambda cols: y_ref[rows, cols], lambda cols: load(rb, cols),
                               write, gain)
            _for_chunks(nb, residual)
        _with_source(first, group, n_ctx_groups, nb, x_refs, epilogue)


def _mix_call(p, o_f, o_b, x_args, ln_g, ln_b, w_s, b_s, mod, out_g, g_post, w_out, layer,
              first, nb, group0, n_groups, n_ctx_groups):
    d = D_MODEL
    rows = nb * CHUNK
    n_tiles = n_groups - group0
    prep = lambda s: group0 + jnp.minimum(s, n_tiles - 1)
    post = lambda s: group0 + jnp.maximum(s - 1, 0)
    kern = functools.partial(_mix_kernel, first=first, nb=nb, group0=group0,
                             n_ctx_groups=n_ctx_groups)
    return pl.pallas_call(
        kern,
        grid=(n_tiles + 1,),
        in_specs=[
            pl.BlockSpec((rows, A_WIDTH), lambda s: (prep(s), COL_U // A_WIDTH)),
            pl.BlockSpec((rows, A_WIDTH), lambda s: (prep(s), COL_V // A_WIDTH)),
            pl.BlockSpec((rows, B_WIDTH), lambda s: (prep(s), 0)),
            pl.BlockSpec((rows, B_WIDTH), lambda s: (prep(s), 0)),
            pl.BlockSpec((rows, B_WIDTH), lambda s: (prep(s), COL_G // B_WIDTH)),
        ] + _x_specs(first, nb, 1, n_ctx_groups, d, post) + [
            pl.BlockSpec((None, 1, A_WIDTH), lambda s: (layer, 0, 0)),
            pl.BlockSpec((None, 1, A_WIDTH), lambda s: (layer, 0, 0)),
            pl.BlockSpec((None, A_HEADS, CHUNK, CHUNK), lambda s: (layer, 0, 0, 0)),
            pl.BlockSpec((None, CHUNK, A_HEADS), lambda s: (layer, 0, 0)),
            pl.BlockSpec((None, MOD_ROWS, 6, d), lambda s: (layer, 0, 0, 0)),
            pl.BlockSpec((None, 1, B_WIDTH), lambda s: (layer, 0, 0)),
            pl.BlockSpec((None, 1, d), lambda s: (layer, 0, 0)),
            pl.BlockSpec((None, A_WIDTH + B_WIDTH, d), lambda s: (layer, 0, 0),
                         pipeline_mode=pl.Buffered(1)),
        ],
        out_specs=pl.BlockSpec((rows, d), lambda s: (post(s), 0)),
        out_shape=jax.ShapeDtypeStruct((n_groups * rows, d), F32),
        scratch_shapes=[pltpu.VMEM((rows, A_WIDTH + B_WIDTH), BF16),
                        pltpu.VMEM((rows, A_WIDTH + B_WIDTH), BF16),
                        pltpu.VMEM((rows, d), F32)],
        input_output_aliases={} if first else {5: 0},
        compiler_params=_params("arbitrary", vmem_limit=HIGH_VMEM_LIMIT),
        name="mix",
    )(p, p, o_f, o_b, p, *x_args, ln_g, ln_b, w_s, b_s, mod, out_g, g_post, w_out)


def _ffn_kernel(x_ref, mod_ref, gpre_ref, gpost_ref, wa_ref, wb_ref, wo_ref, o_ref,
                h_ref, act_ref, *, last, nb, group0, n_ctx_groups):
    j = pl.program_id(1)
    group = pl.program_id(0) + group0

    @pl.when(j == 0)
    def _():
        def body(rb):
            m = _mod_row(group, rb, nb, n_ctx_groups)
            gain = gpre_ref[...] * (1.0 + mod_ref[m, 4:5, :])
            h = _rms(x_ref[_chunk_rows(rb), :]) * gain + mod_ref[m, 3:4, :]
            h_ref[_chunk_rows(rb), :] = h.astype(BF16)
        _for_chunks(nb, body)

    @pl.when(j < FFN_NH)
    def _():
        h = h_ref[...]
        a = jnp.dot(h, wa_ref[...], preferred_element_type=F32)
        b = jnp.dot(h, wb_ref[...], preferred_element_type=F32)
        act_ref[j] = (_silu(a) * b).astype(BF16)

    @pl.when(j == FFN_NH)
    def _():
        for n in range(FFN_NO):
            cols = slice(n * FFN_TN, (n + 1) * FFN_TN)
            acc = jnp.dot(act_ref[0], wo_ref[0:FFN_TH, cols], preferred_element_type=F32)
            for k in range(1, FFN_NH):
                acc += jnp.dot(act_ref[k], wo_ref[k * FFN_TH:(k + 1) * FFN_TH, cols],
                               preferred_element_type=F32)
            if last:
                for b in range(nb):
                    o_ref[b, :, cols] = acc[b * CHUNK:(b + 1) * CHUNK, :]
            else:
                o_ref[:, cols] = acc

        def body(rb):
            rows = _chunk_rows(rb)
            m = _mod_row(group, rb, nb, n_ctx_groups)
            gain = gpost_ref[...] * mod_ref[m, 5:6, :]

            def read_y(cols):
                return o_ref[rb, :, cols] if last else o_ref[rows, cols]

            def write(cols, val):
                if last:
                    o_ref[rb, :, cols] = val
                else:
                    o_ref[rows, cols] = val
            _norm_residual(read_y, lambda cols: x_ref[rows, cols], write, gain)
        _for_chunks(nb, body)


def _ffn_call(xs, mod, g_pre, g_post, w_ffn_in, w_ffn_out, layer, last, nb, group0,
              n_groups, n_ctx_groups):
    d = D_MODEL
    rows = nb * CHUNK
    last_h = FFN_NH - 1
    tile = lambda i: i + group0
    if last:
        out_spec = pl.BlockSpec((nb, CHUNK, d), lambda i, j: (0, i, 0))
        out_shape = jax.ShapeDtypeStruct((nb, (n_groups - group0) * CHUNK, d), F32)
    else:
        out_spec = pl.BlockSpec((rows, d), lambda i, j: (tile(i), 0))
        out_shape = jax.ShapeDtypeStruct((n_groups * rows, d), F32)
    kern = functools.partial(_ffn_kernel, last=last, nb=nb, group0=group0,
                             n_ctx_groups=n_ctx_groups)
    return pl.pallas_call(
        kern,
        grid=(n_groups - group0, FFN_NH + 1),
        in_specs=[
            pl.BlockSpec((rows, d), lambda i, j: (tile(i), 0)),
            pl.BlockSpec((None, MOD_ROWS, 6, d), lambda i, j: (layer, 0, 0, 0)),
            pl.BlockSpec((None, 1, d), lambda i, j: (layer, 0, 0)),
            pl.BlockSpec((None, 1, d), lambda i, j: (layer, 0, 0)),
            pl.BlockSpec((None, d, FFN_TH), lambda i, j: (layer, 0, jnp.minimum(j, last_h))),
            pl.BlockSpec((None, d, FFN_TH),
                         lambda i, j: (layer, 0, FFN_NH + jnp.minimum(j, last_h))),
            pl.BlockSpec((None, FFN_HIDDEN, d), lambda i, j: (layer, 0, 0),
                         pipeline_mode=pl.Buffered(1)),
        ],
        out_specs=out_spec,
        out_shape=out_shape,
        scratch_shapes=[pltpu.VMEM((rows, d), BF16),
                        pltpu.VMEM((FFN_NH, rows, FFN_TH), BF16)],
        input_output_aliases={} if last else {0: 0},
        compiler_params=_params("arbitrary", "arbitrary", vmem_limit=HIGH_VMEM_LIMIT),
        name="ffn",
    )(xs, mod, g_pre, g_post, w_ffn_in, w_ffn_in, w_ffn_out)


def kernel(x, c, ctx, c_ctx, w_mod, b_mod, g_pre_mix, g_post_mix, g_pre_ffn, g_post_ffn, w_in, gmlp_ln_g, gmlp_ln_b, gmlp_ws, gmlp_bs, gla_wd2_fwd, gla_bd_fwd, gla_wd2_bwd, gla_bd_bwd, gla_out_g, w_out, w_ffn_in, w_ffn_out):
    nb, seq, d = x.shape
    ctx_len = ctx.shape[1]
    n_layers = w_mod.shape[0]
    assert d == D_MODEL and nb + 1 <= MOD_ROWS
    assert seq % CHUNK == 0 and ctx_len % CHUNK == 0
    n_ctx_groups = ctx_len // CHUNK
    n_groups = n_ctx_groups + seq // CHUNK

    act = jnp.zeros((MOD_ROWS, d), F32).at[:nb].set(c).at[nb].set(c_ctx)
    mod = _mod_call(act, w_mod, b_mod).reshape(n_layers, MOD_ROWS, 6, d)

    row = lambda t: t.reshape(n_layers, 1, t.shape[-1])
    w_in_b = w_in.astype(BF16)
    w_out_b = w_out.astype(BF16)
    w_ffn_in_b = w_ffn_in.astype(BF16)
    w_ffn_out_b = w_ffn_out.astype(BF16)
    w_s_b = gmlp_ws.astype(BF16)
    b_s_t = jnp.swapaxes(gmlp_bs, 1, 2)
    wd_f = jnp.pad(gla_wd2_fwd, ((0, 0), (0, D_BLOCK - GATE_RANK), (0, 0))).astype(BF16)
    wd_b = jnp.pad(gla_wd2_bwd, ((0, 0), (GATE_RANK, D_BLOCK - 2 * GATE_RANK), (0, 0))).astype(BF16)

    x_args = (x, ctx)
    for l in range(n_layers):
        first = l == 0
        last = l == n_layers - 1
        group0 = n_ctx_groups if last else 0
        p = _inproj_call(x_args, mod, row(g_pre_mix), w_in_b, l, first, nb, n_groups,
                         n_ctx_groups)
        o_f, o_b = _gla_call(p, wd_f, row(gla_bd_fwd), wd_b, row(gla_bd_bwd), l, nb,
                             n_groups, n_ctx_groups)
        xs = _mix_call(p, o_f, o_b, x_args, row(gmlp_ln_g), row(gmlp_ln_b), w_s_b, b_s_t, mod,
                       row(gla_out_g), row(g_post_mix), w_out_b, l, first, nb, group0,
                       n_groups, n_ctx_groups)
        xs = _ffn_call(xs, mod, row(g_pre_ffn), row(g_post_ffn), w_ffn_in_b, w_ffn_out_b,
                       l, last, nb, group0, n_groups, n_ctx_groups)
        x_args = (xs,)
    return xs
```
